```python
import jax, jax.numpy as jnp
from jax import lax
import numpy as np

D_MODEL = 1024
BATCH = 32
SEQ = 2048
DEPTH = 4

EPS = 1e-6
GRID_W = 64

POOL_WINDOWS = (2, 4, 8, 16)
N_POOL_GROUPS = 4
POOL_GROUP_DIM = D_MODEL // 8
POOL_DIM = N_POOL_GROUPS * POOL_GROUP_DIM

ATT_HEADS = 8
ATT_KV_HEADS = 2
ATT_HEAD_DIM = D_MODEL // 16
ATT_Q_DIM = ATT_HEADS * ATT_HEAD_DIM
ATT_KV_DIM = ATT_KV_HEADS * ATT_HEAD_DIM
Q_BLOCK = 128
ROPE_THETA = 10000.0

EVEN_IN_DIM = POOL_DIM + ATT_Q_DIM + 2 * ATT_KV_DIM
EVEN_MIX_DIM = POOL_DIM + ATT_Q_DIM

MLSTM_HEADS = 8
MLSTM_QK_DIM = D_MODEL // 16
MLSTM_V_DIM = D_MODEL // 8
MLSTM_QK_WIDTH = MLSTM_HEADS * MLSTM_QK_DIM
MLSTM_V_WIDTH = MLSTM_HEADS * MLSTM_V_DIM
N_GATES = 4 * MLSTM_HEADS
ODD_IN_DIM = 2 * MLSTM_QK_WIDTH + 2 * MLSTM_V_WIDTH + N_GATES
CONV_K = 5
CHUNK = 128

FFN_HIDDEN = -(-8 * D_MODEL // (3 * 256)) * 256

kernel_name = "hybrid_pool_gqa_mlstm_adaln_encoder"


def rms_norm(x):
    x32 = x.astype(jnp.float32)
    y = x32 * lax.rsqrt(jnp.mean(x32 * x32, axis=-1, keepdims=True) + EPS)
    return y.astype(x.dtype)


def modulate(h, shift, scale):
    return h * (1.0 + scale[:, None, :]) + shift[:, None, :]


def grid_rope(seq_len):
    rows = seq_len // GRID_W
    row = jnp.repeat(jnp.arange(rows), GRID_W).astype(jnp.float32)
    col = jnp.tile(jnp.arange(GRID_W), rows).astype(jnp.float32)
    n_freq = ATT_HEAD_DIM // 4
    inv_freq = ROPE_THETA ** (-jnp.arange(n_freq, dtype=jnp.float32) / n_freq)
    ang = jnp.concatenate([row[:, None] * inv_freq, col[:, None] * inv_freq], axis=-1)
    return jnp.cos(ang), jnp.sin(ang)


def apply_rope(x, cos, sin):
    x32 = x.astype(jnp.float32)
    x1, x2 = x32[..., 0::2], x32[..., 1::2]
    c, s = cos[None, :, None, :], sin[None, :, None, :]
    out = jnp.stack([x1 * c - x2 * s, x1 * s + x2 * c], axis=-1).reshape(x.shape)
    return out.astype(x.dtype)


def multiscale_pool(u):
    B, S, _ = u.shape
    u32 = u.astype(jnp.float32)
    csum = jnp.concatenate([jnp.zeros((B, 1, POOL_DIM), jnp.float32), jnp.cumsum(u32, axis=1)], axis=1)
    t = jnp.arange(S)
    means = []
    for g, w in enumerate(POOL_WINDOWS):
        lo = jnp.clip(t - w // 2, 0, S)
        hi = jnp.clip(t + w - w // 2, 0, S)
        cg = csum[..., g * POOL_GROUP_DIM:(g + 1) * POOL_GROUP_DIM]
        total = jnp.take(cg, hi, axis=1) - jnp.take(cg, lo, axis=1)
        count = (hi - lo).astype(jnp.float32)
        means.append(total / count[None, :, None])
    return (jnp.concatenate(means, axis=-1) - u32).astype(u.dtype)


def grid_attention(q, k, v, cos, sin, q_gain, k_gain):
    q = apply_rope(rms_norm(q) * q_gain, cos, sin)
    k = apply_rope(rms_norm(k) * k_gain, cos, sin)
    B, S, H, Dh = q.shape
    G = H // ATT_KV_HEADS
    nb = S // Q_BLOCK
    qb = q.reshape(B, nb, Q_BLOCK, ATT_KV_HEADS, G, Dh).transpose(1, 0, 3, 4, 2, 5)
    kt = k.transpose(0, 2, 1, 3)
    vt = v.transpose(0, 2, 1, 3)
    scale = Dh ** -0.5

    def block(q_blk):
        s = jnp.einsum('bkgqd,bksd->bkgqs', q_blk, kt).astype(jnp.float32) * scale
        p = jax.nn.softmax(s, axis=-1).astype(vt.dtype)
        return jnp.einsum('bkgqs,bksd->bkgqd', p, vt)

    o = lax.map(block, qb)
    return o.transpose(1, 0, 4, 2, 3, 5).reshape(B, S, H * Dh)


def pool_attention_mixer(h, w_in, w_pool, b_pool, pool_scale, q_gain, k_gain, w_out, cos, sin):
    B, S, _ = h.shape
    z = h @ w_in
    u = z[..., :POOL_DIM]
    q = z[..., POOL_DIM:POOL_DIM + ATT_Q_DIM]
    k = z[..., POOL_DIM + ATT_Q_DIM:POOL_DIM + ATT_Q_DIM + ATT_KV_DIM]
    v = z[..., POOL_DIM + ATT_Q_DIM + ATT_KV_DIM:]
    pooled = multiscale_pool(u).reshape(B, S, N_POOL_GROUPS, POOL_GROUP_DIM)
    a = (jnp.einsum('bsgc,gcd->bsgd', pooled, w_pool) + b_pool).reshape(B, S, POOL_DIM) * pool_scale
    att = grid_attention(q.reshape(B, S, ATT_HEADS, ATT_HEAD_DIM),
                         k.reshape(B, S, ATT_KV_HEADS, ATT_HEAD_DIM),
                         v.reshape(B, S, ATT_KV_HEADS, ATT_HEAD_DIM),
                         cos, sin, q_gain, k_gain)
    return jnp.concatenate([a, att], axis=-1) @ w_out


def centred_depthwise_conv(u, w, b):
    y = lax.conv_general_dilated(u, w[:, None, :].astype(u.dtype), window_strides=(1,),
                                 padding=[(CONV_K // 2, CONV_K // 2)],
                                 dimension_numbers=('NWC', 'WIO', 'NWC'),
                                 feature_group_count=u.shape[-1])
    return y + b


def mlstm_chunkwise(q, k, v, log_i, log_f):
    q, k, v = (a.astype(jnp.float32) for a in (q, k, v))
    B, H, S, Dk = q.shape
    Dv = v.shape[-1]
    nc = S // CHUNK

    def to_chunks(a):
        return jnp.moveaxis(a.reshape(B, H, nc, CHUNK, *a.shape[3:]), 2, 0)

    qc, kc, vc, ic, fc = (to_chunks(a) for a in (q, k, v, log_i, log_f))
    lower = jnp.tril(jnp.ones((CHUNK, CHUNK), bool))

    def step(carry, inp):
        C, n, m = carry
        qj, kj, vj, ij, fj = inp
        b = jnp.cumsum(fj, axis=-1)
        a = b + m[..., None]
        D = jnp.where(lower, b[..., :, None] - b[..., None, :] + ij[..., None, :], -jnp.inf)
        m_t = jnp.maximum(a, jnp.max(D, axis=-1))
        w_inter = jnp.exp(a - m_t)
        s = jnp.einsum('bhtd,bhsd->bhts', qj, kj) * jnp.exp(D - m_t[..., None])
        num = jnp.einsum('bhts,bhsv->bhtv', s, vj) + w_inter[..., None] * jnp.einsum('bhvd,bhtd->bhtv', C, qj)
        den = jnp.sum(s, axis=-1) + w_inter * jnp.einsum('bhd,bhtd->bht', n, qj)
        h = num / jnp.maximum(jnp.abs(den), jnp.exp(-m_t))[..., None]
        bL = b[..., -1]
        g = bL[..., None] - b + ij
        m_new = jnp.maximum(bL + m, jnp.max(g, axis=-1))
        decay = jnp.exp(bL + m - m_new)
        wk = jnp.exp(g - m_new[..., None])
        C = decay[..., None, None] * C + jnp.einsum('bhsv,bhsd->bhvd', vj * wk[..., None], kj)
        n = decay[..., None] * n + jnp.einsum('bhs,bhsd->bhd', wk, kj)
        return (C, n, m_new), h

    init = (jnp.zeros((B, H, Dv, Dk), jnp.float32), jnp.zeros((B, H, Dk), jnp.float32),
            jnp.zeros((B, H), jnp.float32))
    _, hc = lax.scan(step, init, (qc, kc, vc, ic, fc))
    return jnp.moveaxis(hc, 0, 2).reshape(B, H, S, Dv)


def mlstm_mixer(h, w_in, conv_w, conv_b, gate_b, head_gain, w_out):
    B, S, _ = h.shape
    z = h @ w_in
    qk_raw = z[..., :2 * MLSTM_QK_WIDTH]
    v = z[..., 2 * MLSTM_QK_WIDTH:2 * MLSTM_QK_WIDTH + MLSTM_V_WIDTH]
    o = z[..., 2 * MLSTM_QK_WIDTH + MLSTM_V_WIDTH:2 * MLSTM_QK_WIDTH + 2 * MLSTM_V_WIDTH]
    gates = z[..., 2 * MLSTM_QK_WIDTH + 2 * MLSTM_V_WIDTH:]
    qk = jax.nn.silu(centred_depthwise_conv(qk_raw, conv_w, conv_b))
    q = qk[..., :MLSTM_QK_WIDTH].reshape(B, S, MLSTM_HEADS, MLSTM_QK_DIM).transpose(0, 2, 1, 3)
    k = (qk[..., MLSTM_QK_WIDTH:] * MLSTM_QK_DIM ** -0.5).reshape(B, S, MLSTM_HEADS, MLSTM_QK_DIM).transpose(0, 2, 1, 3)
    v = v.reshape(B, S, MLSTM_HEADS, MLSTM_V_DIM).transpose(0, 2, 1, 3)
    g = (gates.astype(jnp.float32) + gate_b.astype(jnp.float32)).reshape(B, S, 2, 2, MLSTM_HEADS)
    g = jnp.transpose(g, (2, 3, 0, 4, 1))
    h_fwd = mlstm_chunkwise(q, k, v, g[0, 0], jax.nn.log_sigmoid(g[0, 1]))
    flip = lambda a: jnp.flip(a, axis=2)
    h_bwd = flip(mlstm_chunkwise(flip(q), flip(k), flip(v), flip(g[1, 0]), flip(jax.nn.log_sigmoid(g[1, 1]))))
    hs = (h_fwd + h_bwd).transpose(0, 2, 1, 3).astype(h.dtype)
    hs = rms_norm(hs) * head_gain
    out = jax.nn.sigmoid(o).reshape(B, S, MLSTM_HEADS, MLSTM_V_DIM) * hs
    return out.reshape(B, S, MLSTM_V_WIDTH) @ w_out


def swiglu(h, w1, w3, w2):
    return (jax.nn.silu(h @ w1) * (h @ w3)) @ w2


def setup_inputs(seed: int = 0) -> dict:
    key = jax.random.key(seed)
    ks = jax.random.split(key, 24)
    n_ev = (DEPTH + 1) // 2
    n_od = DEPTH // 2
    f32 = jnp.float32

    def nrm(k, shape, scale):
        return jax.random.normal(k, shape, f32) * scale

    i_bias = nrm(ks[14], (n_od, 2, MLSTM_HEADS), 0.1)
    f_bias = jnp.linspace(3.0, 6.0, MLSTM_HEADS, dtype=f32)[None, None, :] + nrm(ks[15], (n_od, 2, MLSTM_HEADS), 0.1)
    return {
        'x': nrm(ks[0], (BATCH, SEQ, D_MODEL), 1.0),
        'c': nrm(ks[1], (BATCH, D_MODEL), 1.0),
        'ada_w': nrm(ks[2], (DEPTH, D_MODEL, 6 * D_MODEL), D_MODEL ** -0.5),
        'ada_b': nrm(ks[3], (DEPTH, 6 * D_MODEL), 0.02),
        'ev_w_in': nrm(ks[4], (n_ev, D_MODEL, EVEN_IN_DIM), D_MODEL ** -0.5),
        'ev_w_pool': nrm(ks[5], (n_ev, N_POOL_GROUPS, POOL_GROUP_DIM, POOL_GROUP_DIM), POOL_GROUP_DIM ** -0.5),
        'ev_b_pool': nrm(ks[6], (n_ev, N_POOL_GROUPS, POOL_GROUP_DIM), 0.02),
        'ev_pool_scale': 1.0 + nrm(ks[7], (n_ev, POOL_DIM), 0.1),
        'ev_q_gain': 1.0 + nrm(ks[8], (n_ev, ATT_HEAD_DIM), 0.1),
        'ev_k_gain': 1.0 + nrm(ks[9], (n_ev, ATT_HEAD_DIM), 0.1),
        'ev_w_out': nrm(ks[10], (n_ev, EVEN_MIX_DIM, D_MODEL), EVEN_MIX_DIM ** -0.5),
        'od_w_in': nrm(ks[11], (n_od, D_MODEL, ODD_IN_DIM), D_MODEL ** -0.5),
        'od_conv_w': nrm(ks[12], (n_od, CONV_K, 2 * MLSTM_QK_WIDTH), CONV_K ** -0.5),
        'od_conv_b': nrm(ks[13], (n_od, 2 * MLSTM_QK_WIDTH), 0.02),
        'od_gate_b': jnp.stack([i_bias, f_bias], axis=2).reshape(n_od, N_GATES),
        'od_head_gain': 1.0 + nrm(ks[16], (n_od, MLSTM_HEADS, MLSTM_V_DIM), 0.1),
        'od_w_out': nrm(ks[17], (n_od, MLSTM_V_WIDTH, D_MODEL), MLSTM_V_WIDTH ** -0.5),
        'ffn_w1': nrm(ks[18], (DEPTH, D_MODEL, FFN_HIDDEN), D_MODEL ** -0.5),
        'ffn_w3': nrm(ks[19], (DEPTH, D_MODEL, FFN_HIDDEN), D_MODEL ** -0.5),
        'ffn_w2': nrm(ks[20], (DEPTH, FFN_HIDDEN, D_MODEL), FFN_HIDDEN ** -0.5),
    }


def reference(x, c, ada_w, ada_b, ev_w_in, ev_w_pool, ev_b_pool, ev_pool_scale, ev_q_gain, ev_k_gain,
              ev_w_out, od_w_in, od_conv_w, od_conv_b, od_gate_b, od_head_gain, od_w_out,
              ffn_w1, ffn_w3, ffn_w2):
    cos, sin = grid_rope(x.shape[1])
    cond = jax.nn.silu(c)
    for layer in range(DEPTH):
        mod = cond @ ada_w[layer] + ada_b[layer]
        sh1, sc1, g1, sh2, sc2, g2 = jnp.split(mod, 6, axis=-1)
        h = modulate(rms_norm(x), sh1, sc1)
        j = layer // 2
        if layer % 2 == 0:
            y = pool_attention_mixer(h, ev_w_in[j], ev_w_pool[j], ev_b_pool[j], ev_pool_scale[j],
                                     ev_q_gain[j], ev_k_gain[j], ev_w_out[j], cos, sin)
        else:
            y = mlstm_mixer(h, od_w_in[j], od_conv_w[j], od_conv_b[j], od_gate_b[j],
                            od_head_gain[j], od_w_out[j])
        x = x + g1[:, None, :] * y
        h = modulate(rms_norm(x), sh2, sc2)
        x = x + g2[:, None, :] * swiglu(h, ffn_w1[layer], ffn_w3[layer], ffn_w2[layer])
    return x
```

```python
import functools

import jax
import jax.numpy as jnp
import numpy as np
from jax import lax
from jax.experimental import pallas as pl
from jax.experimental.pallas import tpu as pltpu

F32 = jnp.float32
BF16 = jnp.bfloat16

EPS = 1e-6
GRID_W = 64
ROPE_THETA = 10000.0
POOL_WINDOWS = (2, 4, 8, 16)
POOL_GROUP_DIM = 128
POOL_DIM = 512
ATT_HEADS = 8
ATT_KV_HEADS = 2
ATT_GROUP = ATT_HEADS // ATT_KV_HEADS
ATT_HEAD_DIM = 64
ATT_Q_DIM = 512
ATT_KV_DIM = 128
MLSTM_HEADS = 8
MLSTM_QK_DIM = 64
MLSTM_V_DIM = 128
MLSTM_QK_WIDTH = 512
MLSTM_V_WIDTH = 1024
N_GATES = 32
CONV_K = 5
CHUNK = 128

LANES = 128
POOL_HALO = 16
POOL_TILE = 256
CONV_PAD = 8
VMEM_LIMIT = 56 * 1024 * 1024


def _cparams(*sem):
    return pltpu.CompilerParams(dimension_semantics=sem, vmem_limit_bytes=VMEM_LIMIT)


def _sigmoid(x):
    return 1.0 / (1.0 + jnp.exp(-x))


def _silu(x):
    return x * _sigmoid(x)


def _log_sigmoid(x):
    return jnp.minimum(x, 0.0) - jnp.log(1.0 + jnp.exp(-jnp.abs(x)))


def _norm_mod(x, shift, scale):
    ms = jnp.mean(x * x, axis=-1, keepdims=True)
    return (x * lax.rsqrt(ms + EPS)) * (1.0 + scale) + shift


def _split3(x):
    x1 = x.astype(BF16)
    r1 = x - x1.astype(F32)
    x2 = r1.astype(BF16)
    x3 = (r1 - x2.astype(F32)).astype(BF16)
    return x1, x2, x3


def _dot(a, b):
    return jnp.dot(a, b, preferred_element_type=F32)


def _dot_nt(a, b):
    return lax.dot_general(a, b, (((1,), (1,)), ((), ())), preferred_element_type=F32)


def _ada_kernel(c_ref, w_ref, b_ref, o_ref):
    cond = _silu(c_ref[...]).astype(BF16)
    o_ref[0] = _dot(cond, w_ref[0].astype(BF16)) + b_ref[0]


def _ada_call(c, ada_w, ada_b):
    depth, d, n = ada_w.shape
    b = c.shape[0]
    tn = 1024
    return pl.pallas_call(
        _ada_kernel,
        out_shape=jax.ShapeDtypeStruct((depth, b, n), F32),
        grid=(depth, n // tn),
        in_specs=[
            pl.BlockSpec((b, d), lambda l, j: (0, 0)),
            pl.BlockSpec((1, d, tn), lambda l, j: (l, 0, j)),
            pl.BlockSpec((1, 1, tn), lambda l, j: (l, 0, j)),
        ],
        out_specs=pl.BlockSpec((1, b, tn), lambda l, j: (l, 0, j)),
        compiler_params=_cparams("arbitrary", "arbitrary"),
        name="ada_table",
    )(c, ada_w, ada_b.reshape(depth, 1, n))


def _inproj_even_kernel(x_ref, mod_ref, w_ref, z_ref):
    m = mod_ref[0]
    h = _norm_mod(x_ref[0], m[0:1], m[1:2]).astype(BF16)
    z_ref[0] = _dot(h, w_ref[...]).astype(BF16)


def _inproj_even_call(x, mod, w):
    b, s, d = x.shape
    n = w.shape[1]
    tm = min(512, s)
    return pl.pallas_call(
        _inproj_even_kernel,
        out_shape=jax.ShapeDtypeStruct((b, s, n), BF16),
        grid=(b, s // tm),
        in_specs=[
            pl.BlockSpec((1, tm, d), lambda i, j: (i, j, 0)),
            pl.BlockSpec((1, 6, d), lambda i, j: (i, 0, 0)),
            pl.BlockSpec((d, n), lambda i, j: (0, 0)),
        ],
        out_specs=pl.BlockSpec((1, tm, n), lambda i, j: (i, j, 0)),
        compiler_params=_cparams("parallel", "parallel"),
        name="inproj_even",
    )(x, mod, w)


def _inproj_odd_kernel(x_ref, mod_ref, w_ref, wg_ref, wgt_ref, gb_ref, gbt_ref, z_ref, g_ref, gt_ref):
    m = mod_ref[0]
    h = _norm_mod(x_ref[0], m[0:1], m[1:2]).astype(BF16)
    z_ref[0] = _dot(h, w_ref[...]).astype(BF16)
    g_ref[0] = _dot(h, wg_ref[...]) + gb_ref[...]
    gt_ref[0] = _dot_nt(wgt_ref[...], h) + gbt_ref[...]


def _inproj_odd_call(x, mod, w, wg, gate_b):
    b, s, d = x.shape
    n = w.shape[1]
    tm = min(512, s)
    return pl.pallas_call(
        _inproj_odd_kernel,
        out_shape=(jax.ShapeDtypeStruct((b, s, n), BF16),
                   jax.ShapeDtypeStruct((b, s, N_GATES), F32),
                   jax.ShapeDtypeStruct((b, N_GATES, s), F32)),
        grid=(b, s // tm),
        in_specs=[
            pl.BlockSpec((1, tm, d), lambda i, j: (i, j, 0)),
            pl.BlockSpec((1, 6, d), lambda i, j: (i, 0, 0)),
            pl.BlockSpec((d, n), lambda i, j: (0, 0)),
            pl.BlockSpec((d, N_GATES), lambda i, j: (0, 0)),
            pl.BlockSpec((N_GATES, d), lambda i, j: (0, 0)),
            pl.BlockSpec((1, N_GATES), lambda i, j: (0, 0)),
            pl.BlockSpec((N_GATES, 1), lambda i, j: (0, 0)),
        ],
        out_specs=(pl.BlockSpec((1, tm, n), lambda i, j: (i, j, 0)),
                   pl.BlockSpec((1, tm, N_GATES), lambda i, j: (i, j, 0)),
                   pl.BlockSpec((1, N_GATES, tm), lambda i, j: (i, 0, j))),
        compiler_params=_cparams("parallel", "parallel"),
        name="inproj_odd",
    )(x, mod, w, wg, wg.T, gate_b.reshape(1, N_GATES), gate_b.reshape(N_GATES, 1))


ATT_QB = 128
ATT_PREP = 256


def _seg_mean_sq(x, bd):
    x2 = x * x
    hi = x2.astype(BF16)
    lo = (x2 - hi.astype(F32)).astype(BF16)
    return (_dot(hi, bd) + _dot(lo, bd)) * (1.0 / ATT_HEAD_DIM)


def _rope_swap(x, first_half):
    return jnp.where(first_half, pltpu.roll(x, LANES - 32, 1), pltpu.roll(x, 32, 1))


def _even_mixer_kernel(u_ref, q_ref, kv_ref, cos_ref, sin_ref, qg_ref, kg_ref, band_ref,
                       wp_ref, bp_ref, ps_ref, y_ref, up_s, qs_s, ks_s, vs_s):
    s = u_ref.shape[1]
    lane = lax.broadcasted_iota(jnp.int32, (1, LANES), 1)
    first_half = (lane % ATT_HEAD_DIM) < (ATT_HEAD_DIM // 2)
    r_i = lax.broadcasted_iota(jnp.int32, (LANES, LANES), 0)
    c_i = lax.broadcasted_iota(jnp.int32, (LANES, LANES), 1)
    bd = jnp.where((r_i // ATT_HEAD_DIM) == (c_i // ATT_HEAD_DIM), 1.0, 0.0).astype(BF16)

    def prep(t, carry):
        r0 = pl.multiple_of(t * ATT_PREP, ATT_PREP)
        rows = pl.ds(r0, ATT_PREP)
        cos = cos_ref[rows, :]
        sin = sin_ref[rows, :]

        def norm_rope(xb, gain):
            xb = xb.astype(F32)
            xn = xb * lax.rsqrt(_seg_mean_sq(xb, bd) + EPS) * gain
            return xn * cos + _rope_swap(xn, first_half) * sin

        for cb in range(ATT_Q_DIM // LANES):
            qn = norm_rope(q_ref[0, rows, cb * LANES:(cb + 1) * LANES], qg_ref[...])
            qn = (qn * (ATT_HEAD_DIM ** -0.5)).astype(BF16)
            qs_s[2 * cb, rows, :] = qn[:, :ATT_HEAD_DIM]
            qs_s[2 * cb + 1, rows, :] = qn[:, ATT_HEAD_DIM:]
        kn = norm_rope(kv_ref[0, rows, 0:LANES], kg_ref[...]).astype(BF16)
        ks_s[0, rows, :] = kn[:, :ATT_HEAD_DIM]
        ks_s[1, rows, :] = kn[:, ATT_HEAD_DIM:]
        vv = kv_ref[0, rows, LANES:2 * LANES]
        ones = jnp.ones((ATT_PREP, ATT_HEAD_DIM), BF16)
        vs_s[0, rows, :] = jnp.concatenate([vv[:, :ATT_HEAD_DIM], ones], axis=1)
        vs_s[1, rows, :] = jnp.concatenate([vv[:, ATT_HEAD_DIM:], ones], axis=1)
        return carry

    lax.fori_loop(0, s // ATT_PREP, prep, 0)

    def att_step(j, qb):
        r0 = pl.multiple_of(qb * ATT_QB, ATT_QB)
        rows = pl.ds(r0, ATT_QB)
        qstk = jnp.concatenate([qs_s[ATT_GROUP * j + g, rows, :] for g in range(ATT_GROUP)], axis=0)
        sc = _dot_nt(qstk, ks_s[j])
        mx = jnp.max(sc, axis=-1, keepdims=True)
        p = jnp.exp(sc - mx).astype(BF16)
        oa = _dot(p, vs_s[j])
        on = oa / pltpu.roll(oa, ATT_HEAD_DIM, 1)
        for gp in range(ATT_GROUP // 2):
            a0 = on[(2 * gp) * ATT_QB:(2 * gp + 1) * ATT_QB]
            a1 = on[(2 * gp + 1) * ATT_QB:(2 * gp + 2) * ATT_QB]
            pair = jnp.where(lane < ATT_HEAD_DIM, a0, pltpu.roll(a1, ATT_HEAD_DIM, 1))
            c0 = POOL_DIM + (ATT_GROUP * j + 2 * gp) * ATT_HEAD_DIM
            y_ref[0, rows, c0:c0 + LANES] = pair.astype(BF16)

    for j in range(ATT_KV_HEADS):
        def body(qb, carry, j=j):
            att_step(j, qb)
            return carry
        lax.fori_loop(0, s // ATT_QB, body, 0)

    zeros_h = jnp.zeros((POOL_HALO, POOL_DIM), BF16)
    up_s[0:POOL_HALO, :] = zeros_h
    up_s[POOL_HALO + s:2 * POOL_HALO + s, :] = zeros_h
    up_s[POOL_HALO:POOL_HALO + s, :] = u_ref[0]

    def pool_step(t, carry):
        r0 = pl.multiple_of(t * POOL_TILE, POOL_TILE)
        tok = r0 + lax.broadcasted_iota(jnp.int32, (POOL_TILE, 1), 0)
        for g, w in enumerate(POOL_WINDOWS):
            cols = slice(g * POOL_GROUP_DIM, (g + 1) * POOL_GROUP_DIM)
            slab = up_s[pl.ds(r0, POOL_TILE + 2 * POOL_HALO), cols]
            tot = _dot(band_ref[g], slab)
            lo = jnp.maximum(tok - w // 2, 0)
            hi = jnp.minimum(tok + (w - w // 2), s)
            cnt = (hi - lo).astype(F32)
            ug = u_ref[0, pl.ds(r0, POOL_TILE), cols].astype(F32)
            pooled = (tot / cnt - ug).astype(BF16)
            a = (_dot(pooled, wp_ref[g]) + bp_ref[g]) * ps_ref[g]
            y_ref[0, pl.ds(r0, POOL_TILE), cols] = a.astype(BF16)
        return carry

    lax.fori_loop(0, s // POOL_TILE, pool_step, 0)


def _even_mixer_call(z, cos_t, sin_t, q_gain, k_gain, band, w_pool, b_pool, pool_scale):
    b, s, _ = z.shape
    nq = ATT_Q_DIM // POOL_DIM
    return pl.pallas_call(
        _even_mixer_kernel,
        out_shape=jax.ShapeDtypeStruct((b, s, POOL_DIM + ATT_Q_DIM), BF16),
        grid=(b,),
        in_specs=[
            pl.BlockSpec((1, s, POOL_DIM), lambda i: (i, 0, 0)),
            pl.BlockSpec((1, s, ATT_Q_DIM), lambda i: (i, 0, nq)),
            pl.BlockSpec((1, s, 2 * ATT_KV_DIM), lambda i: (i, 0, (POOL_DIM + ATT_Q_DIM) // (2 * ATT_KV_DIM))),
            pl.BlockSpec((s, LANES), lambda i: (0, 0)),
            pl.BlockSpec((s, LANES), lambda i: (0, 0)),
            pl.BlockSpec((1, LANES), lambda i: (0, 0)),
            pl.BlockSpec((1, LANES), lambda i: (0, 0)),
            pl.BlockSpec(band.shape, lambda i: (0, 0, 0)),
            pl.BlockSpec(w_pool.shape, lambda i: (0, 0, 0)),
            pl.BlockSpec(b_pool.shape, lambda i: (0, 0, 0)),
            pl.BlockSpec(pool_scale.shape, lambda i: (0, 0, 0)),
        ],
        out_specs=pl.BlockSpec((1, s, POOL_DIM + ATT_Q_DIM), lambda i: (i, 0, 0)),
        scratch_shapes=[
            pltpu.VMEM((s + 2 * POOL_HALO, POOL_DIM), BF16),
            pltpu.VMEM((ATT_HEADS, s, ATT_HEAD_DIM), BF16),
            pltpu.VMEM((ATT_KV_HEADS, s, ATT_HEAD_DIM), BF16),
            pltpu.VMEM((ATT_KV_HEADS, s, LANES), BF16),
        ],
        compiler_params=_cparams("parallel"),
        name="even_mixer",
    )(z, z, z, cos_t, sin_t, q_gain, k_gain, band, w_pool, b_pool, pool_scale)


CONV_ROWS = 256
MLSTM_HPG = 2


def _mlstm_kernel(qk_ref, v_ref, o_ref, g_ref, gt_ref, cw_ref, cb_ref, gain_ref, y_ref,
                  pad_s, qs_s, kt_s, bcol_s, brow_s, st_s, m_s, hf_s, hb_s):
    s = qk_ref.shape[1]
    L = CHUNK
    nc = s // L
    r_i = lax.broadcasted_iota(jnp.int32, (L, L), 0)
    c_i = lax.broadcasted_iota(jnp.int32, (L, L), 1)
    low_b = r_i >= c_i
    upp_b = c_i >= r_i
    low01 = jnp.where(low_b, 1.0, 0.0).astype(BF16)
    upp01 = jnp.where(upp_b, 1.0, 0.0).astype(BF16)

    gcol = lax.broadcasted_iota(jnp.int32, (1, N_GATES), 1)
    grow = lax.broadcasted_iota(jnp.int32, (N_GATES, 1), 0)
    for c in range(nc):
        rows = slice(c * L, (c + 1) * L)
        lf = _split3(_log_sigmoid(g_ref[0, rows, :]))
        pre = sum(_dot(low01, t) for t in lf)
        suf = sum(_dot(upp01, t) for t in lf)
        bcol_s[rows, :] = jnp.where(gcol < N_GATES // 2, pre, suf)
        lft = _split3(_log_sigmoid(gt_ref[0, :, rows]))
        pre_r = sum(_dot(t, upp01) for t in lft)
        suf_r = sum(_dot(t, low01) for t in lft)
        brow_s[:, rows] = jnp.where(grow < N_GATES // 2, pre_r, suf_r)

    zpad = jnp.zeros((CONV_PAD, LANES), F32)
    pad_s[0:CONV_PAD, :] = zpad
    pad_s[CONV_PAD + s:2 * CONV_PAD + s, :] = zpad
    for cb in range(2 * MLSTM_QK_WIDTH // LANES):
        cols = slice(cb * LANES, (cb + 1) * LANES)
        pad_s[CONV_PAD:CONV_PAD + s, :] = qk_ref[0, :, cols].astype(F32)
        cw = cw_ref[:, cols]
        cbias = cb_ref[:, cols]
        for t in range(s // CONV_ROWS):
            r0 = t * CONV_ROWS
            acc = cbias + cw[0:1] * pad_s[r0 + CONV_PAD - 2:r0 + CONV_PAD - 2 + CONV_ROWS, :]
            for j in range(1, CONV_K):
                off = r0 + CONV_PAD - CONV_K // 2 + j
                acc = acc + cw[j:j + 1] * pad_s[off:off + CONV_ROWS, :]
            val = _silu(acc)
            if cb < MLSTM_QK_WIDTH // LANES:
                vb = val.astype(BF16)
                qs_s[2 * cb, r0:r0 + CONV_ROWS, :] = vb[:, :MLSTM_QK_DIM]
                qs_s[2 * cb + 1, r0:r0 + CONV_ROWS, :] = vb[:, MLSTM_QK_DIM:]
            else:
                kt = (val * (MLSTM_QK_DIM ** -0.5)).T
                kt_s[cb - MLSTM_QK_WIDTH // LANES, :, r0:r0 + CONV_ROWS] = kt.astype(BF16)

    ones_v = jnp.ones((L, MLSTM_V_DIM), BF16)

    def chunk_step(h, d, ci):
        c = ci if d == 0 else nc - 1 - ci
        r0 = pl.multiple_of(c * L, L)
        rows = pl.ds(r0, L)
        sidx = 2 * h + d
        jf = d * 16 + 8 + h
        ji = d * 16 + h
        state = st_s[sidx]
        m = m_s[sidx]
        q_c = qs_s[h, rows, :]
        hk = (h % 2) * MLSTM_QK_DIM
        kt_c = kt_s[h // 2, hk:hk + MLSTM_QK_DIM, rows]
        v_aug = jnp.concatenate([v_ref[0, rows, h * MLSTM_V_DIM:(h + 1) * MLSTM_V_DIM], ones_v], axis=1)
        b_col = bcol_s[rows, jf:jf + 1]
        b_row = brow_s[jf:jf + 1, rows]
        i_row = gt_ref[0, ji:ji + 1, rows]
        b_tot = b_row[:, L - 1:L] if d == 0 else b_row[:, 0:1]
        a = b_col + m
        dmat = jnp.where(low_b if d == 0 else upp_b, b_col - (b_row - i_row), -jnp.inf)
        m_t = jnp.maximum(a, jnp.max(dmat, axis=-1, keepdims=True))
        w_inter = jnp.exp(a - m_t)
        sm = _dot(q_c, kt_c) * jnp.exp(dmat - m_t)
        r = _dot(sm.astype(BF16), v_aug) + w_inter * _dot(q_c, state.astype(BF16))
        num = r[:, :MLSTM_V_DIM]
        den = r[:, MLSTM_V_DIM:]
        hval = num / jnp.maximum(jnp.abs(den), jnp.exp(-m_t))
        if d == 0:
            hf_s[h % MLSTM_HPG, rows, :] = hval
        else:
            hb_s[h % MLSTM_HPG, rows, :] = hval
        g_row = b_tot - b_row + i_row
        m_new = jnp.maximum(b_tot + m, jnp.max(g_row, axis=-1, keepdims=True))
        decay = jnp.exp(b_tot + m - m_new)
        wk = jnp.exp(g_row - m_new)
        kw = (kt_c.astype(F32) * wk).astype(BF16)
        st_s[sidx] = decay * state + _dot(kw, v_aug)
        m_s[sidx] = m_new

    for hg in range(MLSTM_HEADS // MLSTM_HPG):
        heads = range(hg * MLSTM_HPG, (hg + 1) * MLSTM_HPG)
        for h in heads:
            for d in range(2):
                st_s[2 * h + d] = jnp.zeros((MLSTM_QK_DIM, 2 * MLSTM_V_DIM), F32)
                m_s[2 * h + d] = jnp.zeros((1, 1), F32)

        def body(ci, carry, heads=heads):
            for h in heads:
                for d in range(2):
                    chunk_step(h, d, ci)
            return carry

        lax.fori_loop(0, nc, body, 0)

        for h in heads:
            cols = slice(h * MLSTM_V_DIM, (h + 1) * MLSTM_V_DIM)

            def fin(t, carry, h=h, cols=cols):
                r0 = pl.multiple_of(t * CONV_ROWS, CONV_ROWS)
                rows = pl.ds(r0, CONV_ROWS)
                hs = hf_s[h % MLSTM_HPG, rows, :] + hb_s[h % MLSTM_HPG, rows, :]
                hn = hs * lax.rsqrt(jnp.mean(hs * hs, axis=-1, keepdims=True) + EPS) * gain_ref[:, cols]
                og = _sigmoid(o_ref[0, rows, cols].astype(F32))
                y_ref[0, rows, cols] = (og * hn).astype(BF16)
                return carry

            lax.fori_loop(0, s // CONV_ROWS, fin, 0)


def _mlstm_call(z, g, gt, conv_w, conv_b, head_gain):
    b, s, _ = z.shape
    vw = MLSTM_V_WIDTH
    return pl.pallas_call(
        _mlstm_kernel,
        out_shape=jax.ShapeDtypeStruct((b, s, vw), BF16),
        grid=(b,),
        in_specs=[
            pl.BlockSpec((1, s, 2 * MLSTM_QK_WIDTH), lambda i: (i, 0, 0)),
            pl.BlockSpec((1, s, vw), lambda i: (i, 0, 1)),
            pl.BlockSpec((1, s, vw), lambda i: (i, 0, 2)),
            pl.BlockSpec((1, s, N_GATES), lambda i: (i, 0, 0)),
            pl.BlockSpec((1, N_GATES, s), lambda i: (i, 0, 0)),
            pl.BlockSpec(conv_w.shape, lambda i: (0, 0)),
            pl.BlockSpec(conv_b.shape, lambda i: (0, 0)),
            pl.BlockSpec(head_gain.shape, lambda i: (0, 0)),
        ],
        out_specs=pl.BlockSpec((1, s, vw), lambda i: (i, 0, 0)),
        scratch_shapes=[
            pltpu.VMEM((s + 2 * CONV_PAD, LANES), F32),
            pltpu.VMEM((MLSTM_HEADS, s, MLSTM_QK_DIM), BF16),
            pltpu.VMEM((MLSTM_HEADS // 2, LANES, s), BF16),
            pltpu.VMEM((s, N_GATES), F32),
            pltpu.VMEM((N_GATES, s), F32),
            pltpu.VMEM((2 * MLSTM_HEADS, MLSTM_QK_DIM, 2 * MLSTM_V_DIM), F32),
            pltpu.VMEM((2 * MLSTM_HEADS, 1, 1), F32),
            pltpu.VMEM((MLSTM_HPG, s, MLSTM_V_DIM), F32),
            pltpu.VMEM((MLSTM_HPG, s, MLSTM_V_DIM), F32),
        ],
        compiler_params=_cparams("parallel"),
        name="mlstm_mixer",
    )(z, z, z, g, gt, conv_w, conv_b, head_gain)


FFN_CHUNK = 256


def _post_kernel(x_ref, y_ref, mod_ref, wo_ref, w1_ref, w3_ref, w2_ref, o_ref, acc_s):
    m = mod_ref[0]
    g1, sh2, sc2, g2 = m[2:3], m[3:4], m[4:5], m[5:6]
    x1 = x_ref[0] + g1 * _dot(y_ref[0], wo_ref[...])
    h = _norm_mod(x1, sh2, sc2).astype(BF16)
    n_chunks = w1_ref.shape[0]

    def body(j, carry):
        a = _dot(h, w1_ref[j])
        b = _dot(h, w3_ref[j])
        gact = (_silu(a) * b).astype(BF16)
        contrib = _dot(gact, w2_ref[j])

        @pl.when(j == 0)
        def _():
            acc_s[...] = contrib

        @pl.when(j > 0)
        def _():
            acc_s[...] += contrib

        return carry

    lax.fori_loop(0, n_chunks, body, 0)
    o_ref[0] = x1 + g2 * acc_s[...]


def _post_call(x, y, mod, w_out, w1, w3, w2):
    b, s, d = x.shape
    tm = min(512, s)
    const3 = lambda i, j: (0, 0, 0)
    return pl.pallas_call(
        _post_kernel,
        out_shape=jax.ShapeDtypeStruct((b, s, d), F32),
        grid=(b, s // tm),
        in_specs=[
            pl.BlockSpec((1, tm, d), lambda i, j: (i, j, 0)),
            pl.BlockSpec((1, tm, y.shape[2]), lambda i, j: (i, j, 0)),
            pl.BlockSpec((1, 6, d), lambda i, j: (i, 0, 0)),
            pl.BlockSpec(w_out.shape, lambda i, j: (0, 0)),
            pl.BlockSpec(w1.shape, const3, pipeline_mode=pl.Buffered(1)),
            pl.BlockSpec(w3.shape, const3, pipeline_mode=pl.Buffered(1)),
            pl.BlockSpec(w2.shape, const3, pipeline_mode=pl.Buffered(1)),
        ],
        out_specs=pl.BlockSpec((1, tm, d), lambda i, j: (i, j, 0)),
        scratch_shapes=[pltpu.VMEM((tm, d), F32)],
        compiler_params=_cparams("parallel", "parallel"),
        name="post_ffn",
    )(x, y, mod, w_out, w1, w3, w2)


def _rope_tables(s):
    rows = s // GRID_W
    row = jnp.repeat(jnp.arange(rows), GRID_W).astype(F32)
    col = jnp.tile(jnp.arange(GRID_W), rows).astype(F32)
    n_freq = ATT_HEAD_DIM // 4
    inv_freq = ROPE_THETA ** (-jnp.arange(n_freq, dtype=F32) / n_freq)
    ang = jnp.concatenate([row[:, None] * inv_freq, col[:, None] * inv_freq], axis=-1)
    cos, sin = jnp.cos(ang), jnp.sin(ang)
    cos_t = jnp.tile(cos, (1, LANES // cos.shape[1]))
    sin_t = jnp.tile(jnp.concatenate([-sin, sin], axis=-1), (1, LANES // ATT_HEAD_DIM))
    return cos_t, sin_t


def _pool_band():
    r = np.arange(POOL_TILE)[:, None]
    c = np.arange(POOL_TILE + 2 * POOL_HALO)[None, :]
    bands = []
    for w in POOL_WINDOWS:
        off = c - POOL_HALO - r + w // 2
        bands.append((off >= 0) & (off < w))
    return jnp.asarray(np.stack(bands), dtype=BF16)


def _head_split_perm():
    return np.concatenate([np.arange(0, ATT_HEAD_DIM, 2), np.arange(1, ATT_HEAD_DIM, 2)])


def kernel(x, c, ada_w, ada_b, ev_w_in, ev_w_pool, ev_b_pool, ev_pool_scale, ev_q_gain, ev_k_gain,
           ev_w_out, od_w_in, od_conv_w, od_conv_b, od_gate_b, od_head_gain, od_w_out,
           ffn_w1, ffn_w3, ffn_w2):
    b, s, d = x.shape
    depth = ada_w.shape[0]
    mod = _ada_call(c, ada_w, ada_b).reshape(depth, b, 6, d)
    cos_t, sin_t = _rope_tables(s)
    band = _pool_band()
    perm = _head_split_perm()
    qk_cols = np.concatenate([POOL_DIM + hh * ATT_HEAD_DIM + perm for hh in range(ATT_HEADS + ATT_KV_HEADS)])
    col_perm = np.concatenate([np.arange(POOL_DIM), qk_cols, np.arange(POOL_DIM + ATT_Q_DIM + ATT_KV_DIM,
                                                                      POOL_DIM + ATT_Q_DIM + 2 * ATT_KV_DIM)])
    hidden = ffn_w1.shape[2]
    n_ch = hidden // FFN_CHUNK

    for layer in range(depth):
        j = layer // 2
        if layer % 2 == 0:
            w_in = ev_w_in[j][:, col_perm].astype(BF16)
            z = _inproj_even_call(x, mod[layer], w_in)
            qg = jnp.tile(ev_q_gain[j][perm], LANES // ATT_HEAD_DIM).reshape(1, LANES)
            kg = jnp.tile(ev_k_gain[j][perm], LANES // ATT_HEAD_DIM).reshape(1, LANES)
            y = _even_mixer_call(z, cos_t, sin_t, qg, kg, band, ev_w_pool[j].astype(BF16),
                                 ev_b_pool[j].reshape(-1, 1, POOL_GROUP_DIM),
                                 ev_pool_scale[j].reshape(-1, 1, POOL_GROUP_DIM))
            w_out = ev_w_out[j].astype(BF16)
        else:
            n_main = 2 * MLSTM_QK_WIDTH + 2 * MLSTM_V_WIDTH
            w_in = od_w_in[j][:, :n_main].astype(BF16)
            wg = od_w_in[j][:, n_main:].astype(BF16)
            z, g, gt = _inproj_odd_call(x, mod[layer], w_in, wg, od_gate_b[j])
            y = _mlstm_call(z, g, gt, od_conv_w[j], od_conv_b[j].reshape(1, -1),
                            od_head_gain[j].reshape(1, -1))
            w_out = od_w_out[j].astype(BF16)
        w1 = ffn_w1[layer].astype(BF16).reshape(d, n_ch, FFN_CHUNK).transpose(1, 0, 2)
        w3 = ffn_w3[layer].astype(BF16).reshape(d, n_ch, FFN_CHUNK).transpose(1, 0, 2)
        w2 = ffn_w2[layer].astype(BF16).reshape(n_ch, FFN_CHUNK, d)
        x = _post_call(x, y, mod[layer], w_out, w1, w3, w2)
    return x
```

```python
import jax
import jax.numpy as jnp
import numpy as np
from jax import lax
from jax.experimental import pallas as pl
from jax.experimental.pallas import tpu as pltpu

F32 = jnp.float32
BF16 = jnp.bfloat16

EPS = 1e-6
GRID_W = 64
ROPE_THETA = 10000.0
POOL_WINDOWS = (2, 4, 8, 16)
POOL_GROUP_DIM = 128
POOL_DIM = 512
ATT_HEADS = 8
ATT_KV_HEADS = 2
ATT_GROUP = ATT_HEADS // ATT_KV_HEADS
ATT_HEAD_DIM = 64
ATT_Q_DIM = 512
ATT_KV_DIM = 128
MLSTM_HEADS = 8
MLSTM_QK_DIM = 64
MLSTM_V_DIM = 128
MLSTM_QK_WIDTH = 512
MLSTM_V_WIDTH = 1024
N_GATES = 32
CONV_K = 5
CHUNK = 128

LANES = 128
POOL_HALO = 16
POOL_TILE = 256
CONV_PAD = 8
VMEM_LIMIT = 56 * 1024 * 1024


def _cparams(*sem):
    return pltpu.CompilerParams(dimension_semantics=sem, vmem_limit_bytes=VMEM_LIMIT)


def _sigmoid(x):
    return 1.0 / (1.0 + jnp.exp(-x))


def _silu(x):
    return x * _sigmoid(x)


def _log_sigmoid(x):
    return jnp.minimum(x, 0.0) - jnp.log(1.0 + jnp.exp(-jnp.abs(x)))


def _norm_mod(x, shift, scale):
    ms = jnp.mean(x * x, axis=-1, keepdims=True)
    return (x * lax.rsqrt(ms + EPS)) * (1.0 + scale) + shift


def _split3(x):
    x1 = x.astype(BF16)
    r1 = x - x1.astype(F32)
    x2 = r1.astype(BF16)
    x3 = (r1 - x2.astype(F32)).astype(BF16)
    return x1, x2, x3


def _dot(a, b):
    return jnp.dot(a, b, preferred_element_type=F32)


def _dot_nt(a, b):
    return lax.dot_general(a, b, (((1,), (1,)), ((), ())), preferred_element_type=F32)


def _ada_kernel(c_ref, w_ref, b_ref, o_ref):
    cond = _silu(c_ref[...]).astype(BF16)
    o_ref[0] = _dot(cond, w_ref[0].astype(BF16)) + b_ref[0]


def _ada_call(c, ada_w, ada_b):
    depth, d, n = ada_w.shape
    b = c.shape[0]
    tn = 1024
    return pl.pallas_call(
        _ada_kernel,
        out_shape=jax.ShapeDtypeStruct((depth, b, n), F32),
        grid=(depth, n // tn),
        in_specs=[
            pl.BlockSpec((b, d), lambda l, j: (0, 0)),
            pl.BlockSpec((1, d, tn), lambda l, j: (l, 0, j)),
            pl.BlockSpec((1, 1, tn), lambda l, j: (l, 0, j)),
        ],
        out_specs=pl.BlockSpec((1, b, tn), lambda l, j: (l, 0, j)),
        compiler_params=_cparams("arbitrary", "arbitrary"),
        name="ada_table",
    )(c, ada_w, ada_b.reshape(depth, 1, n))


def _inproj_even_kernel(x_ref, mod_ref, w_ref, z_ref):
    m = mod_ref[0]
    h = _norm_mod(x_ref[0], m[0:1], m[1:2]).astype(BF16)
    z_ref[0] = _dot(h, w_ref[...]).astype(BF16)


def _inproj_even_call(x, mod, w):
    b, s, d = x.shape
    n = w.shape[1]
    tm = min(512, s)
    return pl.pallas_call(
        _inproj_even_kernel,
        out_shape=jax.ShapeDtypeStruct((b, s, n), BF16),
        grid=(b, s // tm),
        in_specs=[
            pl.BlockSpec((1, tm, d), lambda i, j: (i, j, 0)),
            pl.BlockSpec((1, 6, d), lambda i, j: (i, 0, 0)),
            pl.BlockSpec((d, n), lambda i, j: (0, 0)),
        ],
        out_specs=pl.BlockSpec((1, tm, n), lambda i, j: (i, j, 0)),
        compiler_params=_cparams("parallel", "parallel"),
        name="inproj_even",
    )(x, mod, w)


def _inproj_odd_kernel(x_ref, mod_ref, w_ref, wg_ref, wgt_ref, gb_ref, gbt_ref, z_ref, g_ref, gt_ref):
    m = mod_ref[0]
    h = _norm_mod(x_ref[0], m[0:1], m[1:2]).astype(BF16)
    z_ref[0] = _dot(h, w_ref[...]).astype(BF16)
    g_ref[0] = _dot(h, wg_ref[...]) + gb_ref[...]
    gt_ref[0] = _dot_nt(wgt_ref[...], h) + gbt_ref[...]


def _inproj_odd_call(x, mod, w, wg, gate_b):
    b, s, d = x.shape
    n = w.shape[1]
    tm = min(512, s)
    return pl.pallas_call(
        _inproj_odd_kernel,
        out_shape=(jax.ShapeDtypeStruct((b, s, n), BF16),
                   jax.ShapeDtypeStruct((b, s, N_GATES), F32),
                   jax.ShapeDtypeStruct((b, N_GATES, s), F32)),
        grid=(b, s // tm),
        in_specs=[
            pl.BlockSpec((1, tm, d), lambda i, j: (i, j, 0)),
            pl.BlockSpec((1, 6, d), lambda i, j: (i, 0, 0)),
            pl.BlockSpec((d, n), lambda i, j: (0, 0)),
            pl.BlockSpec((d, N_GATES), lambda i, j: (0, 0)),
            pl.BlockSpec((N_GATES, d), lambda i, j: (0, 0)),
            pl.BlockSpec((1, N_GATES), lambda i, j: (0, 0)),
            pl.BlockSpec((N_GATES, 1), lambda i, j: (0, 0)),
        ],
        out_specs=(pl.BlockSpec((1, tm, n), lambda i, j: (i, j, 0)),
                   pl.BlockSpec((1, tm, N_GATES), lambda i, j: (i, j, 0)),
                   pl.BlockSpec((1, N_GATES, tm), lambda i, j: (i, 0, j))),
        compiler_params=_cparams("parallel", "parallel"),
        name="inproj_odd",
    )(x, mod, w, wg, wg.T, gate_b.reshape(1, N_GATES), gate_b.reshape(N_GATES, 1))


ATT_QB = 256
ATT_PREP = 256
LOG2E = 1.4426950408889634


def _seg_mean_sq(x, bd):
    x2 = x * x
    hi = x2.astype(BF16)
    lo = (x2 - hi.astype(F32)).astype(BF16)
    return (_dot(hi, bd) + _dot(lo, bd)) * (1.0 / ATT_HEAD_DIM)


def _rope_swap(x, first_half):
    return jnp.where(first_half, pltpu.roll(x, LANES - 32, 1), pltpu.roll(x, 32, 1))


def _even_mixer_kernel(u_ref, q_ref, kv_ref, cos_ref, sin_ref, qg_ref, kg_ref, band_ref,
                       wp_ref, bp_ref, ps_ref, y_ref, up_s, qs_s, ks_s, vt_s):
    s = u_ref.shape[1]
    lane = lax.broadcasted_iota(jnp.int32, (1, LANES), 1)
    first_half = (lane % ATT_HEAD_DIM) < (ATT_HEAD_DIM // 2)
    r_i = lax.broadcasted_iota(jnp.int32, (LANES, LANES), 0)
    c_i = lax.broadcasted_iota(jnp.int32, (LANES, LANES), 1)
    bd = jnp.where((r_i // ATT_HEAD_DIM) == (c_i // ATT_HEAD_DIM), 1.0, 0.0).astype(BF16)

    def prep(t, carry):
        r0 = pl.multiple_of(t * ATT_PREP, ATT_PREP)
        rows = pl.ds(r0, ATT_PREP)
        cos = cos_ref[rows, :]
        sin = sin_ref[rows, :]

        def norm_rope(xb, gain):
            xb = xb.astype(F32)
            xn = xb * lax.rsqrt(_seg_mean_sq(xb, bd) + EPS) * gain
            return xn * cos + _rope_swap(xn, first_half) * sin

        for cb in range(ATT_Q_DIM // LANES):
            qn = norm_rope(q_ref[0, rows, cb * LANES:(cb + 1) * LANES], qg_ref[...])
            qn = (qn * (ATT_HEAD_DIM ** -0.5 * LOG2E)).astype(BF16)
            qs_s[2 * cb, rows, :] = qn[:, :ATT_HEAD_DIM]
            qs_s[2 * cb + 1, rows, :] = qn[:, ATT_HEAD_DIM:]
        kn = norm_rope(kv_ref[0, rows, 0:LANES], kg_ref[...]).astype(BF16)
        ks_s[0, rows, :] = kn[:, :ATT_HEAD_DIM]
        ks_s[1, rows, :] = kn[:, ATT_HEAD_DIM:]
        vt = kv_ref[0, rows, LANES:2 * LANES].astype(F32).T.astype(BF16)
        ones = jnp.ones((ATT_HEAD_DIM, ATT_PREP), BF16)
        vt_s[0, :, rows] = jnp.concatenate([vt[:ATT_HEAD_DIM], ones], axis=0)
        vt_s[1, :, rows] = jnp.concatenate([vt[ATT_HEAD_DIM:], ones], axis=0)
        return carry

    lax.fori_loop(0, s // ATT_PREP, prep, 0)

    def att_step(j, gp, qb):
        r0 = pl.multiple_of(qb * ATT_QB, ATT_QB)
        rows = pl.ds(r0, ATT_QB)
        h0 = ATT_GROUP * j + 2 * gp
        qstk = jnp.concatenate([qs_s[h0, rows, :], qs_s[h0 + 1, rows, :]], axis=0)
        st = _dot_nt(ks_s[j], qstk)
        mx = jnp.max(st, axis=0, keepdims=True)
        p = jnp.exp2(st - mx).astype(BF16)
        outs = []
        for g in range(2):
            oa = _dot(vt_s[j], p[:, g * ATT_QB:(g + 1) * ATT_QB])
            outs.append(oa[:ATT_HEAD_DIM] / oa[ATT_HEAD_DIM:])
        pair = jnp.concatenate(outs, axis=0).T
        c0 = POOL_DIM + h0 * ATT_HEAD_DIM
        y_ref[0, rows, c0:c0 + LANES] = pair.astype(BF16)

    for gp in range(ATT_GROUP // 2):
        def body(qb, carry, gp=gp):
            for j in range(ATT_KV_HEADS):
                att_step(j, gp, qb)
            return carry
        lax.fori_loop(0, s // ATT_QB, body, 0)

    zeros_h = jnp.zeros((POOL_HALO, POOL_DIM), BF16)
    up_s[0:POOL_HALO, :] = zeros_h
    up_s[POOL_HALO + s:2 * POOL_HALO + s, :] = zeros_h
    up_s[POOL_HALO:POOL_HALO + s, :] = u_ref[0]

    def pool_step(t, carry):
        r0 = pl.multiple_of(t * POOL_TILE, POOL_TILE)
        tok = r0 + lax.broadcasted_iota(jnp.int32, (POOL_TILE, 1), 0)
        for g, w in enumerate(POOL_WINDOWS):
            cols = slice(g * POOL_GROUP_DIM, (g + 1) * POOL_GROUP_DIM)
            slab = up_s[pl.ds(r0, POOL_TILE + 2 * POOL_HALO), cols]
            tot = _dot(band_ref[g], slab)
            lo = jnp.maximum(tok - w // 2, 0)
            hi = jnp.minimum(tok + (w - w // 2), s)
            cnt = (hi - lo).astype(F32)
            ug = u_ref[0, pl.ds(r0, POOL_TILE), cols].astype(F32)
            pooled = (tot / cnt - ug).astype(BF16)
            a = (_dot(pooled, wp_ref[g]) + bp_ref[g]) * ps_ref[g]
            y_ref[0, pl.ds(r0, POOL_TILE), cols] = a.astype(BF16)
        return carry

    lax.fori_loop(0, s // POOL_TILE, pool_step, 0)


def _even_mixer_call(z, cos_t, sin_t, q_gain, k_gain, band, w_pool, b_pool, pool_scale):
    b, s, _ = z.shape
    q_blk = POOL_DIM // ATT_Q_DIM
    kv_blk = (POOL_DIM + ATT_Q_DIM) // (2 * ATT_KV_DIM)
    return pl.pallas_call(
        _even_mixer_kernel,
        out_shape=jax.ShapeDtypeStruct((b, s, POOL_DIM + ATT_Q_DIM), BF16),
        grid=(b,),
        in_specs=[
            pl.BlockSpec((1, s, POOL_DIM), lambda i: (i, 0, 0)),
            pl.BlockSpec((1, s, ATT_Q_DIM), lambda i: (i, 0, q_blk)),
            pl.BlockSpec((1, s, 2 * ATT_KV_DIM), lambda i: (i, 0, kv_blk)),
            pl.BlockSpec((s, LANES), lambda i: (0, 0)),
            pl.BlockSpec((s, LANES), lambda i: (0, 0)),
            pl.BlockSpec((1, LANES), lambda i: (0, 0)),
            pl.BlockSpec((1, LANES), lambda i: (0, 0)),
            pl.BlockSpec(band.shape, lambda i: (0, 0, 0)),
            pl.BlockSpec(w_pool.shape, lambda i: (0, 0, 0)),
            pl.BlockSpec(b_pool.shape, lambda i: (0, 0, 0)),
            pl.BlockSpec(pool_scale.shape, lambda i: (0, 0, 0)),
        ],
        out_specs=pl.BlockSpec((1, s, POOL_DIM + ATT_Q_DIM), lambda i: (i, 0, 0)),
        scratch_shapes=[
            pltpu.VMEM((s + 2 * POOL_HALO, POOL_DIM), BF16),
            pltpu.VMEM((ATT_HEADS, s, ATT_HEAD_DIM), BF16),
            pltpu.VMEM((ATT_KV_HEADS, s, ATT_HEAD_DIM), BF16),
            pltpu.VMEM((ATT_KV_HEADS, LANES, s), BF16),
        ],
        compiler_params=_cparams("parallel"),
        name="even_mixer",
    )(z, z, z, cos_t, sin_t, q_gain, k_gain, band, w_pool, b_pool, pool_scale)


CONV_ROWS = 256
MLSTM_HPG = 4
MLSTM_VA = MLSTM_V_DIM + 16


def _mlstm_kernel(qk_ref, v_ref, o_ref, g_ref, gt_ref, cw_ref, cb_ref, gain_ref, y_ref,
                  pad_s, qs_s, ks_s, vt_s, ibcol_s, brow_s, st_s, m_s, ht_s):
    s = qk_ref.shape[1]
    L = CHUNK
    nc = s // L
    nh = MLSTM_HEADS
    r_i = lax.broadcasted_iota(jnp.int32, (L, L), 0)
    c_i = lax.broadcasted_iota(jnp.int32, (L, L), 1)
    low_b = r_i >= c_i
    upp_b = c_i >= r_i
    low01 = jnp.where(low_b, 1.0, 0.0).astype(BF16)
    upp01 = jnp.where(upp_b, 1.0, 0.0).astype(BF16)

    gcol = lax.broadcasted_iota(jnp.int32, (1, N_GATES), 1)
    grow = lax.broadcasted_iota(jnp.int32, (N_GATES, 1), 0)
    for c in range(nc):
        rows = slice(c * L, (c + 1) * L)
        gc = g_ref[0, rows, :]
        lf = _split3(_log_sigmoid(gc))
        pre = sum(_dot(low01, t) for t in lf)
        suf = sum(_dot(upp01, t) for t in lf)
        bc = jnp.where(gcol < N_GATES // 2, pre, suf)
        ibcol_s[rows, :] = jnp.concatenate(
            [gc[:, 0:nh] - bc[:, nh:2 * nh], gc[:, 2 * nh:3 * nh] - bc[:, 3 * nh:4 * nh]], axis=1)
        lft = _split3(_log_sigmoid(gt_ref[0, :, rows]))
        pre_r = sum(_dot(t, upp01) for t in lft)
        suf_r = sum(_dot(t, low01) for t in lft)
        brow_s[:, rows] = jnp.where(grow < N_GATES // 2, pre_r, suf_r)

    zpad = jnp.zeros((CONV_PAD, LANES), F32)
    pad_s[0:CONV_PAD, :] = zpad
    pad_s[CONV_PAD + s:2 * CONV_PAD + s, :] = zpad
    n_qb = MLSTM_QK_WIDTH // LANES
    for cb in range(2 * n_qb):
        cols = slice(cb * LANES, (cb + 1) * LANES)
        pad_s[CONV_PAD:CONV_PAD + s, :] = qk_ref[0, :, cols].astype(F32)
        cw = cw_ref[:, cols]
        cbias = cb_ref[:, cols]
        for t in range(s // CONV_ROWS):
            r0 = t * CONV_ROWS
            acc = cbias + cw[0:1] * pad_s[r0 + CONV_PAD - 2:r0 + CONV_PAD - 2 + CONV_ROWS, :]
            for j in range(1, CONV_K):
                off = r0 + CONV_PAD - CONV_K // 2 + j
                acc = acc + cw[j:j + 1] * pad_s[off:off + CONV_ROWS, :]
            val = _silu(acc)
            if cb < n_qb:
                vb = val.astype(BF16)
                qs_s[2 * cb, r0:r0 + CONV_ROWS, :] = vb[:, :MLSTM_QK_DIM]
                qs_s[2 * cb + 1, r0:r0 + CONV_ROWS, :] = vb[:, MLSTM_QK_DIM:]
            else:
                vb = (val * (MLSTM_QK_DIM ** -0.5)).astype(BF16)
                ks_s[2 * (cb - n_qb), r0:r0 + CONV_ROWS, :] = vb[:, :MLSTM_QK_DIM]
                ks_s[2 * (cb - n_qb) + 1, r0:r0 + CONV_ROWS, :] = vb[:, MLSTM_QK_DIM:]

    ones_rows = jnp.ones((MLSTM_VA - MLSTM_V_DIM, CONV_ROWS), BF16)

    def vt_step(t, carry):
        r0 = pl.multiple_of(t * CONV_ROWS, CONV_ROWS)
        rows = pl.ds(r0, CONV_ROWS)
        for h in range(nh):
            vv = v_ref[0, rows, h * MLSTM_V_DIM:(h + 1) * MLSTM_V_DIM]
            vt_s[h, 0:MLSTM_V_DIM, rows] = vv.astype(F32).T.astype(BF16)
            vt_s[h, MLSTM_V_DIM:MLSTM_VA, rows] = ones_rows
        return carry

    lax.fori_loop(0, s // CONV_ROWS, vt_step, 0)

    def chunk_step(h, d, ci):
        c = ci if d == 0 else nc - 1 - ci
        r0 = pl.multiple_of(c * L, L)
        rows = pl.ds(r0, L)
        sidx = d * nh + h
        jf = d * 2 * nh + nh + h
        ji = d * 2 * nh + h
        state = st_s[sidx]
        m = m_s[sidx]
        q_c = qs_s[h, rows, :]
        k_c = ks_s[h, rows, :]
        vt_c = vt_s[h, :, rows]
        ib_col = ibcol_s[rows, sidx:sidx + 1]
        b_row = brow_s[jf:jf + 1, rows]
        i_row = gt_ref[0, ji:ji + 1, rows]
        b_tot = b_row[:, L - 1:L] if d == 0 else b_row[:, 0:1]
        a_row = b_row + m
        dt = jnp.where(upp_b if d == 0 else low_b, b_row + ib_col, -jnp.inf)
        m_t = jnp.maximum(a_row, jnp.max(dt, axis=0, keepdims=True))
        w_inter = jnp.exp(a_row - m_t)
        kq = _dot_nt(jnp.concatenate([k_c, state.astype(BF16)], axis=0), q_c)
        st = kq[:L] * jnp.exp(dt - m_t)
        r = _dot(vt_c, st.astype(BF16)) + kq[L:] * w_inter
        den = r[MLSTM_V_DIM:MLSTM_V_DIM + 1]
        scale = 1.0 / jnp.maximum(jnp.abs(den), jnp.exp(-m_t))
        ht_s[h % MLSTM_HPG, :, rows] += r[:MLSTM_V_DIM] * scale
        g_row = b_tot - b_row + i_row
        m_new = jnp.maximum(b_tot + m, jnp.max(g_row, axis=-1, keepdims=True))
        decay = jnp.exp(b_tot + m - m_new)
        wk = jnp.exp(g_row - m_new)
        vw = (vt_c.astype(F32) * wk).astype(BF16)
        st_s[sidx] = decay * state + _dot(vw, k_c)
        m_s[sidx] = m_new

    for hg in range(nh // MLSTM_HPG):
        heads = range(hg * MLSTM_HPG, (hg + 1) * MLSTM_HPG)
        for h in heads:
            for d in range(2):
                st_s[d * nh + h] = jnp.zeros((MLSTM_VA, MLSTM_QK_DIM), F32)
                m_s[d * nh + h] = jnp.zeros((1, 1), F32)
        for h in heads:
            ht_s[h % MLSTM_HPG] = jnp.zeros((MLSTM_V_DIM, s), F32)

        def body(ci, carry, heads=heads):
            for h in heads:
                for d in range(2):
                    chunk_step(h, d, ci)
            return carry

        lax.fori_loop(0, nc, body, 0)

        for h in heads:
            cols = slice(h * MLSTM_V_DIM, (h + 1) * MLSTM_V_DIM)

            def fin(t, carry, h=h, cols=cols):
                r0 = pl.multiple_of(t * CONV_ROWS, CONV_ROWS)
                rows = pl.ds(r0, CONV_ROWS)
                hs = ht_s[h % MLSTM_HPG, :, rows].T
                hn = hs * lax.rsqrt(jnp.mean(hs * hs, axis=-1, keepdims=True) + EPS) * gain_ref[:, cols]
                og = _sigmoid(o_ref[0, rows, cols].astype(F32))
                y_ref[0, rows, cols] = (og * hn).astype(BF16)
                return carry

            lax.fori_loop(0, s // CONV_ROWS, fin, 0)


def _mlstm_call(z, g, gt, conv_w, conv_b, head_gain):
    b, s, _ = z.shape
    vw = MLSTM_V_WIDTH
    once = pl.Buffered(1)
    return pl.pallas_call(
        _mlstm_kernel,
        out_shape=jax.ShapeDtypeStruct((b, s, vw), BF16),
        grid=(b,),
        in_specs=[
            pl.BlockSpec((1, s, 2 * MLSTM_QK_WIDTH), lambda i: (i, 0, 0), pipeline_mode=once),
            pl.BlockSpec((1, s, vw), lambda i: (i, 0, 1), pipeline_mode=once),
            pl.BlockSpec((1, s, vw), lambda i: (i, 0, 2), pipeline_mode=once),
            pl.BlockSpec((1, s, N_GATES), lambda i: (i, 0, 0)),
            pl.BlockSpec((1, N_GATES, s), lambda i: (i, 0, 0)),
            pl.BlockSpec(conv_w.shape, lambda i: (0, 0)),
            pl.BlockSpec(conv_b.shape, lambda i: (0, 0)),
            pl.BlockSpec(head_gain.shape, lambda i: (0, 0)),
        ],
        out_specs=pl.BlockSpec((1, s, vw), lambda i: (i, 0, 0)),
        scratch_shapes=[
            pltpu.VMEM((s + 2 * CONV_PAD, LANES), F32),
            pltpu.VMEM((MLSTM_HEADS, s, MLSTM_QK_DIM), BF16),
            pltpu.VMEM((MLSTM_HEADS, s, MLSTM_QK_DIM), BF16),
            pltpu.VMEM((MLSTM_HEADS, MLSTM_VA, s), BF16),
            pltpu.VMEM((s, 2 * MLSTM_HEADS), F32),
            pltpu.VMEM((N_GATES, s), F32),
            pltpu.VMEM((2 * MLSTM_HEADS, MLSTM_VA, MLSTM_QK_DIM), F32),
            pltpu.VMEM((2 * MLSTM_HEADS, 1, 1), F32),
            pltpu.VMEM((MLSTM_HPG, MLSTM_V_DIM, s), F32),
        ],
        compiler_params=_cparams("parallel"),
        name="mlstm_mixer",
    )(z, z, z, g, gt, conv_w, conv_b, head_gain)


FFN_SLABS = ((0, 1280), (1280, 2816))


def _post_kernel(x_ref, y_ref, mod_ref, wo_ref, w1_ref, w3_ref, w2_ref, o_ref):
    m = mod_ref[0]
    g1, sh2, sc2, g2 = m[2:3], m[3:4], m[4:5], m[5:6]
    x1 = x_ref[0] + g1 * _dot(y_ref[0], wo_ref[...])
    h = _norm_mod(x1, sh2, sc2).astype(BF16)
    acc = None
    for lo, hi in FFN_SLABS:
        a = _dot(h, w1_ref[:, lo:hi])
        b = _dot(h, w3_ref[:, lo:hi])
        gact = (_silu(a) * b).astype(BF16)
        contrib = _dot(gact, w2_ref[lo:hi, :])
        acc = contrib if acc is None else acc + contrib
    o_ref[0] = x1 + g2 * acc


def _post_call(x, y, mod, w_out, w1, w3, w2):
    b, s, d = x.shape
    tm = min(512, s)
    const2 = lambda i, j: (0, 0)
    assert FFN_SLABS[-1][1] == w1.shape[1]
    return pl.pallas_call(
        _post_kernel,
        out_shape=jax.ShapeDtypeStruct((b, s, d), F32),
        grid=(b, s // tm),
        in_specs=[
            pl.BlockSpec((1, tm, d), lambda i, j: (i, j, 0)),
            pl.BlockSpec((1, tm, y.shape[2]), lambda i, j: (i, j, 0)),
            pl.BlockSpec((1, 6, d), lambda i, j: (i, 0, 0)),
            pl.BlockSpec(w_out.shape, lambda i, j: (0, 0)),
            pl.BlockSpec(w1.shape, const2, pipeline_mode=pl.Buffered(1)),
            pl.BlockSpec(w3.shape, const2, pipeline_mode=pl.Buffered(1)),
            pl.BlockSpec(w2.shape, const2, pipeline_mode=pl.Buffered(1)),
        ],
        out_specs=pl.BlockSpec((1, tm, d), lambda i, j: (i, j, 0)),
        compiler_params=_cparams("parallel", "parallel"),
        name="post_ffn",
    )(x, y, mod, w_out, w1, w3, w2)


def _rope_tables(s):
    rows = s // GRID_W
    row = jnp.repeat(jnp.arange(rows), GRID_W).astype(F32)
    col = jnp.tile(jnp.arange(GRID_W), rows).astype(F32)
    n_freq = ATT_HEAD_DIM // 4
    inv_freq = ROPE_THETA ** (-jnp.arange(n_freq, dtype=F32) / n_freq)
    ang = jnp.concatenate([row[:, None] * inv_freq, col[:, None] * inv_freq], axis=-1)
    cos, sin = jnp.cos(ang), jnp.sin(ang)
    cos_t = jnp.tile(cos, (1, LANES // cos.shape[1]))
    sin_t = jnp.tile(jnp.concatenate([-sin, sin], axis=-1), (1, LANES // ATT_HEAD_DIM))
    return cos_t, sin_t


def _pool_band():
    r = np.arange(POOL_TILE)[:, None]
    c = np.arange(POOL_TILE + 2 * POOL_HALO)[None, :]
    bands = []
    for w in POOL_WINDOWS:
        off = c - POOL_HALO - r + w // 2
        bands.append((off >= 0) & (off < w))
    return jnp.asarray(np.stack(bands), dtype=BF16)


def _head_split_perm():
    return np.concatenate([np.arange(0, ATT_HEAD_DIM, 2), np.arange(1, ATT_HEAD_DIM, 2)])


def kernel(x, c, ada_w, ada_b, ev_w_in, ev_w_pool, ev_b_pool, ev_pool_scale, ev_q_gain, ev_k_gain,
           ev_w_out, od_w_in, od_conv_w, od_conv_b, od_gate_b, od_head_gain, od_w_out,
           ffn_w1, ffn_w3, ffn_w2):
    b, s, d = x.shape
    depth = ada_w.shape[0]
    mod = _ada_call(c, ada_w, ada_b).reshape(depth, b, 6, d)
    cos_t, sin_t = _rope_tables(s)
    band = _pool_band()
    perm = _head_split_perm()
    qk_cols = np.concatenate([POOL_DIM + hh * ATT_HEAD_DIM + perm for hh in range(ATT_HEADS + ATT_KV_HEADS)])
    col_perm = np.concatenate([np.arange(POOL_DIM), qk_cols, np.arange(POOL_DIM + ATT_Q_DIM + ATT_KV_DIM,
                                                                      POOL_DIM + ATT_Q_DIM + 2 * ATT_KV_DIM)])

    for layer in range(depth):
        j = layer // 2
        if layer % 2 == 0:
            w_in = ev_w_in[j][:, col_perm].astype(BF16)
            z = _inproj_even_call(x, mod[layer], w_in)
            qg = jnp.tile(ev_q_gain[j][perm], LANES // ATT_HEAD_DIM).reshape(1, LANES)
            kg = jnp.tile(ev_k_gain[j][perm], LANES // ATT_HEAD_DIM).reshape(1, LANES)
            y = _even_mixer_call(z, cos_t, sin_t, qg, kg, band, ev_w_pool[j].astype(BF16),
                                 ev_b_pool[j].reshape(-1, 1, POOL_GROUP_DIM),
                                 ev_pool_scale[j].reshape(-1, 1, POOL_GROUP_DIM))
            w_out = ev_w_out[j].astype(BF16)
        else:
            n_main = 2 * MLSTM_QK_WIDTH + 2 * MLSTM_V_WIDTH
            w_in = od_w_in[j][:, :n_main].astype(BF16)
            wg = od_w_in[j][:, n_main:].astype(BF16)
            z, g, gt = _inproj_odd_call(x, mod[layer], w_in, wg, od_gate_b[j])
            y = _mlstm_call(z, g, gt, od_conv_w[j], od_conv_b[j].reshape(1, -1),
                            od_head_gain[j].reshape(1, -1))
            w_out = od_w_out[j].astype(BF16)
        x = _post_call(x, y, mod[layer], w_out, ffn_w1[layer].astype(BF16), ffn_w3[layer].astype(BF16),
                       ffn_w2[layer].astype(BF16))
    return x
```

```python
import jax
import jax.numpy as jnp
import numpy as np
from jax import lax
from jax.experimental import pallas as pl
from jax.experimental.pallas import tpu as pltpu

F32 = jnp.float32
BF16 = jnp.bfloat16

EPS = 1e-6
GRID_W = 64
ROPE_THETA = 10000.0
POOL_WINDOWS = (2, 4, 8, 16)
POOL_GROUP_DIM = 128
POOL_DIM = 512
ATT_HEADS = 8
ATT_KV_HEADS = 2
ATT_GROUP = ATT_HEADS // ATT_KV_HEADS
ATT_HEAD_DIM = 64
ATT_Q_DIM = 512
ATT_KV_DIM = 128
MLSTM_HEADS = 8
MLSTM_QK_DIM = 64
MLSTM_V_DIM = 128
MLSTM_QK_WIDTH = 512
MLSTM_V_WIDTH = 1024
N_GATES = 32
CONV_K = 5

LANES = 128
POOL_HALO = 16
POOL_TILE = 256
CONV_PAD = 8
VMEM_LIMIT = 56 * 1024 * 1024


def _cparams(*sem):
    return pltpu.CompilerParams(dimension_semantics=sem, vmem_limit_bytes=VMEM_LIMIT)


def _sigmoid(x):
    return 1.0 / (1.0 + jnp.exp(-x))


def _silu(x):
    return x * _sigmoid(x)


def _log_sigmoid(x):
    return jnp.minimum(x, 0.0) - jnp.log(1.0 + jnp.exp(-jnp.abs(x)))


def _norm_mod(x, shift, scale):
    ms = jnp.mean(x * x, axis=-1, keepdims=True)
    return (x * lax.rsqrt(ms + EPS)) * (1.0 + scale) + shift


def _split3(x):
    x1 = x.astype(BF16)
    r1 = x - x1.astype(F32)
    x2 = r1.astype(BF16)
    x3 = (r1 - x2.astype(F32)).astype(BF16)
    return x1, x2, x3


def _dot(a, b):
    return jnp.dot(a, b, preferred_element_type=F32)


def _dot_nt(a, b):
    return lax.dot_general(a, b, (((1,), (1,)), ((), ())), preferred_element_type=F32)


def _ada_kernel(c_ref, w_ref, b_ref, o_ref):
    cond = _silu(c_ref[...]).astype(BF16)
    o_ref[0] = _dot(cond, w_ref[0].astype(BF16)) + b_ref[0]


def _ada_call(c, ada_w, ada_b):
    depth, d, n = ada_w.shape
    b = c.shape[0]
    tn = 1024
    return pl.pallas_call(
        _ada_kernel,
        out_shape=jax.ShapeDtypeStruct((depth, b, n), F32),
        grid=(depth, n // tn),
        in_specs=[
            pl.BlockSpec((b, d), lambda l, j: (0, 0)),
            pl.BlockSpec((1, d, tn), lambda l, j: (l, 0, j)),
            pl.BlockSpec((1, 1, tn), lambda l, j: (l, 0, j)),
        ],
        out_specs=pl.BlockSpec((1, b, tn), lambda l, j: (l, 0, j)),
        compiler_params=_cparams("arbitrary", "arbitrary"),
        name="ada_table",
    )(c, ada_w, ada_b.reshape(depth, 1, n))


def _inproj_even_kernel(x_ref, mod_ref, w_ref, z_ref):
    m = mod_ref[0]
    h = _norm_mod(x_ref[0], m[0:1], m[1:2]).astype(BF16)
    z_ref[0] = _dot(h, w_ref[...]).astype(BF16)


def _inproj_even_call(x, mod, w):
    b, s, d = x.shape
    n = w.shape[1]
    tm = min(512, s)
    return pl.pallas_call(
        _inproj_even_kernel,
        out_shape=jax.ShapeDtypeStruct((b, s, n), BF16),
        grid=(b, s // tm),
        in_specs=[
            pl.BlockSpec((1, tm, d), lambda i, j: (i, j, 0)),
            pl.BlockSpec((1, 6, d), lambda i, j: (i, 0, 0)),
            pl.BlockSpec((d, n), lambda i, j: (0, 0)),
        ],
        out_specs=pl.BlockSpec((1, tm, n), lambda i, j: (i, j, 0)),
        compiler_params=_cparams("parallel", "parallel"),
        name="inproj_even",
    )(x, mod, w)


def _inproj_odd_kernel(x_ref, mod_ref, w_ref, wgt_ref, gbt_ref, z_ref, gt_ref):
    m = mod_ref[0]
    h = _norm_mod(x_ref[0], m[0:1], m[1:2]).astype(BF16)
    z_ref[0] = _dot(h, w_ref[...]).astype(BF16)
    gt_ref[0] = _dot_nt(wgt_ref[...], h) + gbt_ref[...]


def _inproj_odd_call(x, mod, w, wg, gate_b):
    b, s, d = x.shape
    n = w.shape[1]
    tm = min(512, s)
    return pl.pallas_call(
        _inproj_odd_kernel,
        out_shape=(jax.ShapeDtypeStruct((b, s, n), BF16),
                   jax.ShapeDtypeStruct((b, N_GATES, s), F32)),
        grid=(b, s // tm),
        in_specs=[
            pl.BlockSpec((1, tm, d), lambda i, j: (i, j, 0)),
            pl.BlockSpec((1, 6, d), lambda i, j: (i, 0, 0)),
            pl.BlockSpec((d, n), lambda i, j: (0, 0)),
            pl.BlockSpec((N_GATES, d), lambda i, j: (0, 0)),
            pl.BlockSpec((N_GATES, 1), lambda i, j: (0, 0)),
        ],
        out_specs=(pl.BlockSpec((1, tm, n), lambda i, j: (i, j, 0)),
                   pl.BlockSpec((1, N_GATES, tm), lambda i, j: (i, 0, j))),
        compiler_params=_cparams("parallel", "parallel"),
        name="inproj_odd",
    )(x, mod, w, wg.T, gate_b.reshape(N_GATES, 1))


ATT_QB = 256
ATT_PREP = 256
LOG2E = 1.4426950408889634
ATT_UNROLL = 4
ATT_VA = ATT_HEAD_DIM + 16


def _seg_mean_sq(x, bd):
    x2 = x * x
    hi = x2.astype(BF16)
    lo = (x2 - hi.astype(F32)).astype(BF16)
    return (_dot(hi, bd) + _dot(lo, bd)) * (1.0 / ATT_HEAD_DIM)


def _rope_swap(x, first_half):
    return jnp.where(first_half, pltpu.roll(x, LANES - 32, 1), pltpu.roll(x, 32, 1))


def _even_mixer_kernel(u_ref, q_ref, kv_ref, cos_ref, sin_ref, qg_ref, kg_ref, band_ref,
                       wp_ref, bp_ref, ps_ref, y_ref, up_s, qs_s, ks_s, vt_s, st_s, mx_s):
    s = u_ref.shape[1]
    lane = lax.broadcasted_iota(jnp.int32, (1, LANES), 1)
    first_half = (lane % ATT_HEAD_DIM) < (ATT_HEAD_DIM // 2)
    r_i = lax.broadcasted_iota(jnp.int32, (LANES, LANES), 0)
    c_i = lax.broadcasted_iota(jnp.int32, (LANES, LANES), 1)
    bd = jnp.where((r_i // ATT_HEAD_DIM) == (c_i // ATT_HEAD_DIM), 1.0, 0.0).astype(BF16)

    def prep(t, carry):
        r0 = pl.multiple_of(t * ATT_PREP, ATT_PREP)
        rows = pl.ds(r0, ATT_PREP)
        cos = cos_ref[rows, :]
        sin = sin_ref[rows, :]

        def norm_rope(xb, gain):
            xb = xb.astype(F32)
            xn = xb * lax.rsqrt(_seg_mean_sq(xb, bd) + EPS) * gain
            return xn * cos + _rope_swap(xn, first_half) * sin

        for cb in range(ATT_Q_DIM // LANES):
            qn = norm_rope(q_ref[0, rows, cb * LANES:(cb + 1) * LANES], qg_ref[...])
            qn = (qn * (ATT_HEAD_DIM ** -0.5 * LOG2E)).astype(BF16)
            qs_s[2 * cb, rows, :] = qn[:, :ATT_HEAD_DIM]
            qs_s[2 * cb + 1, rows, :] = qn[:, ATT_HEAD_DIM:]
        kn = norm_rope(kv_ref[0, rows, 0:LANES], kg_ref[...]).astype(BF16)
        ks_s[0, rows, :] = kn[:, :ATT_HEAD_DIM]
        ks_s[1, rows, :] = kn[:, ATT_HEAD_DIM:]
        vt = kv_ref[0, rows, LANES:2 * LANES].astype(F32).T.astype(BF16)
        ones = jnp.ones((ATT_VA - ATT_HEAD_DIM, ATT_PREP), BF16)
        vt_s[0, :, rows] = jnp.concatenate([vt[:ATT_HEAD_DIM], ones], axis=0)
        vt_s[1, :, rows] = jnp.concatenate([vt[ATT_HEAD_DIM:], ones], axis=0)
        return carry

    lax.fori_loop(0, s // ATT_PREP, prep, 0)

    n_qb = s // ATT_QB
    n_steps = ATT_KV_HEADS * (ATT_GROUP // 2) * n_qb

    def decode(t):
        j = t // ((ATT_GROUP // 2) * n_qb)
        gp = (t // n_qb) % (ATT_GROUP // 2)
        rows = pl.ds(pl.multiple_of((t % n_qb) * ATT_QB, ATT_QB), ATT_QB)
        return j, ATT_GROUP * j + 2 * gp, rows

    def scores(t, slot):
        j, h0, rows = decode(t)
        qstk = jnp.concatenate([qs_s[h0, rows, :], qs_s[h0 + 1, rows, :]], axis=0)
        st = _dot_nt(ks_s[j], qstk)
        st_s[slot] = st
        mx_s[slot] = jnp.max(st, axis=0, keepdims=True)

    def weighted_values(t, slot):
        j, h0, rows = decode(t)
        p = jnp.exp2(st_s[slot] - mx_s[slot]).astype(BF16)
        outs = []
        for g in range(2):
            oa = _dot(vt_s[j], p[:, g * ATT_QB:(g + 1) * ATT_QB])
            outs.append(oa[:ATT_HEAD_DIM] / oa[ATT_HEAD_DIM:ATT_HEAD_DIM + 1])
        pair = jnp.concatenate(outs, axis=0).T
        c0 = pl.multiple_of(POOL_DIM + h0 * ATT_HEAD_DIM, LANES)
        y_ref[0, rows, pl.ds(c0, LANES)] = pair.astype(BF16)

    scores(0, 0)

    def att_body(k, carry):
        t0 = ATT_UNROLL * k
        for u in range(ATT_UNROLL):
            scores(jnp.minimum(t0 + u + 1, n_steps - 1), (u + 1) % 2)
            weighted_values(t0 + u, u % 2)
        return carry

    lax.fori_loop(0, n_steps // ATT_UNROLL, att_body, 0)

    zeros_h = jnp.zeros((POOL_HALO, POOL_DIM), BF16)
    up_s[0:POOL_HALO, :] = zeros_h
    up_s[POOL_HALO + s:2 * POOL_HALO + s, :] = zeros_h
    up_s[POOL_HALO:POOL_HALO + s, :] = u_ref[0]

    def pool_step(t, carry):
        r0 = pl.multiple_of(t * POOL_TILE, POOL_TILE)
        tok = r0 + lax.broadcasted_iota(jnp.int32, (POOL_TILE, 1), 0)
        for g, w in enumerate(POOL_WINDOWS):
            cols = slice(g * POOL_GROUP_DIM, (g + 1) * POOL_GROUP_DIM)
            slab = up_s[pl.ds(r0, POOL_TILE + 2 * POOL_HALO), cols]
            tot = _dot(band_ref[g], slab)
            lo = jnp.maximum(tok - w // 2, 0)
            hi = jnp.minimum(tok + (w - w // 2), s)
            cnt = (hi - lo).astype(F32)
            ug = u_ref[0, pl.ds(r0, POOL_TILE), cols].astype(F32)
            pooled = (tot / cnt - ug).astype(BF16)
            a = (_dot(pooled, wp_ref[g]) + bp_ref[g]) * ps_ref[g]
            y_ref[0, pl.ds(r0, POOL_TILE), cols] = a.astype(BF16)
        return carry

    lax.fori_loop(0, s // POOL_TILE, pool_step, 0)


def _even_mixer_call(z, cos_t, sin_t, q_gain, k_gain, band, w_pool, b_pool, pool_scale):
    b, s, _ = z.shape
    q_blk = POOL_DIM // ATT_Q_DIM
    kv_blk = (POOL_DIM + ATT_Q_DIM) // (2 * ATT_KV_DIM)
    return pl.pallas_call(
        _even_mixer_kernel,
        out_shape=jax.ShapeDtypeStruct((b, s, POOL_DIM + ATT_Q_DIM), BF16),
        grid=(b,),
        in_specs=[
            pl.BlockSpec((1, s, POOL_DIM), lambda i: (i, 0, 0)),
            pl.BlockSpec((1, s, ATT_Q_DIM), lambda i: (i, 0, q_blk)),
            pl.BlockSpec((1, s, 2 * ATT_KV_DIM), lambda i: (i, 0, kv_blk)),
            pl.BlockSpec((s, LANES), lambda i: (0, 0)),
            pl.BlockSpec((s, LANES), lambda i: (0, 0)),
            pl.BlockSpec((1, LANES), lambda i: (0, 0)),
            pl.BlockSpec((1, LANES), lambda i: (0, 0)),
            pl.BlockSpec(band.shape, lambda i: (0, 0, 0)),
            pl.BlockSpec(w_pool.shape, lambda i: (0, 0, 0)),
            pl.BlockSpec(b_pool.shape, lambda i: (0, 0, 0)),
            pl.BlockSpec(pool_scale.shape, lambda i: (0, 0, 0)),
        ],
        out_specs=pl.BlockSpec((1, s, POOL_DIM + ATT_Q_DIM), lambda i: (i, 0, 0)),
        scratch_shapes=[
            pltpu.VMEM((s + 2 * POOL_HALO, POOL_DIM), BF16),
            pltpu.VMEM((ATT_HEADS, s, ATT_HEAD_DIM), BF16),
            pltpu.VMEM((ATT_KV_HEADS, s, ATT_HEAD_DIM), BF16),
            pltpu.VMEM((ATT_KV_HEADS, ATT_VA, s), BF16),
            pltpu.VMEM((2, s, 2 * ATT_QB), F32),
            pltpu.VMEM((2, 1, 2 * ATT_QB), F32),
        ],
        compiler_params=_cparams("parallel"),
        name="even_mixer",
    )(z, z, z, cos_t, sin_t, q_gain, k_gain, band, w_pool, b_pool, pool_scale)


CONV_ROWS = 256
MLSTM_HPG = 4
MLSTM_VA = MLSTM_V_DIM + 16
MLSTM_CHUNK = 256


def _mlstm_kernel(qk_ref, v_ref, o_ref, gt_ref, cw_ref, cb_ref, gain_ref, y_ref,
                  pad_s, qs_s, ks_s, vt_s, ibcol_s, brow_s, st_s, m_s, ht_s):
    s = qk_ref.shape[1]
    L = MLSTM_CHUNK
    nc = s // L
    nh = MLSTM_HEADS
    r_i = lax.broadcasted_iota(jnp.int32, (L, L), 0)
    c_i = lax.broadcasted_iota(jnp.int32, (L, L), 1)
    low_b = r_i >= c_i
    upp_b = c_i >= r_i
    low01 = jnp.where(low_b, 1.0, 0.0).astype(BF16)
    upp01 = jnp.where(upp_b, 1.0, 0.0).astype(BF16)
    eye01 = jnp.where(r_i == c_i, 1.0, 0.0).astype(BF16)

    grow = lax.broadcasted_iota(jnp.int32, (N_GATES, 1), 0)
    for c in range(nc):
        cols = slice(c * L, (c + 1) * L)
        gtc = gt_ref[0, :, cols]
        lft = _split3(_log_sigmoid(gtc) * LOG2E)
        pre_r = sum(_dot(t, upp01) for t in lft)
        suf_r = sum(_dot(t, low01) for t in lft)
        bq = jnp.where(grow < N_GATES // 2, pre_r, suf_r)
        brow_s[:, cols] = bq
        i2 = gtc * LOG2E
        ib = jnp.concatenate([i2[0:nh] - bq[nh:2 * nh], i2[2 * nh:3 * nh] - bq[3 * nh:4 * nh]], axis=0)
        ibcol_s[cols, :] = sum(_dot_nt(eye01, t) for t in _split3(ib))

    zpad = jnp.zeros((CONV_PAD, LANES), F32)
    pad_s[0:CONV_PAD, :] = zpad
    pad_s[CONV_PAD + s:2 * CONV_PAD + s, :] = zpad
    n_qb = MLSTM_QK_WIDTH // LANES
    for cb in range(2 * n_qb):
        cols = slice(cb * LANES, (cb + 1) * LANES)
        pad_s[CONV_PAD:CONV_PAD + s, :] = qk_ref[0, :, cols].astype(F32)
        cw = cw_ref[:, cols]
        cbias = cb_ref[:, cols]
        for t in range(s // CONV_ROWS):
            r0 = t * CONV_ROWS
            acc = cbias + cw[0:1] * pad_s[r0 + CONV_PAD - 2:r0 + CONV_PAD - 2 + CONV_ROWS, :]
            for j in range(1, CONV_K):
                off = r0 + CONV_PAD - CONV_K // 2 + j
                acc = acc + cw[j:j + 1] * pad_s[off:off + CONV_ROWS, :]
            val = _silu(acc)
            if cb < n_qb:
                vb = val.astype(BF16)
                qs_s[2 * cb, r0:r0 + CONV_ROWS, :] = vb[:, :MLSTM_QK_DIM]
                qs_s[2 * cb + 1, r0:r0 + CONV_ROWS, :] = vb[:, MLSTM_QK_DIM:]
            else:
                vb = (val * (MLSTM_QK_DIM ** -0.5)).astype(BF16)
                ks_s[2 * (cb - n_qb), r0:r0 + CONV_ROWS, :] = vb[:, :MLSTM_QK_DIM]
                ks_s[2 * (cb - n_qb) + 1, r0:r0 + CONV_ROWS, :] = vb[:, MLSTM_QK_DIM:]

    ones_rows = jnp.ones((MLSTM_VA - MLSTM_V_DIM, CONV_ROWS), BF16)

    def vt_step(t, carry):
        r0 = pl.multiple_of(t * CONV_ROWS, CONV_ROWS)
        rows = pl.ds(r0, CONV_ROWS)
        for h in range(nh):
            vv = v_ref[0, rows, h * MLSTM_V_DIM:(h + 1) * MLSTM_V_DIM]
            vt_s[h, 0:MLSTM_V_DIM, rows] = vv.astype(F32).T.astype(BF16)
            vt_s[h, MLSTM_V_DIM:MLSTM_VA, rows] = ones_rows
        return carry

    lax.fori_loop(0, s // CONV_ROWS, vt_step, 0)

    def chunk_scores(h, d, ci):
        c = ci if d == 0 else nc - 1 - ci
        rows = pl.ds(pl.multiple_of(c * L, L), L)
        sidx = d * nh + h
        state = st_s[sidx]
        q_c = qs_s[h, rows, :]
        k_c = ks_s[h, rows, :]
        kq = _dot_nt(jnp.concatenate([k_c, state.astype(BF16)], axis=0), q_c)
        return rows, state, k_c, kq

    def chunk_finish(h, d, rows, state, k_c, kq):
        sidx = d * nh + h
        jf = d * 2 * nh + nh + h
        ji = d * 2 * nh + h
        m = m_s[sidx]
        vt_c = vt_s[h, :, rows]
        ib_col = ibcol_s[rows, sidx:sidx + 1]
        b_row = brow_s[jf:jf + 1, rows]
        i_row = gt_ref[0, ji:ji + 1, rows] * LOG2E
        b_tot = b_row[:, L - 1:L] if d == 0 else b_row[:, 0:1]
        a_row = b_row + m
        dt = jnp.where(upp_b if d == 0 else low_b, b_row + ib_col, -jnp.inf)
        m_t = jnp.maximum(a_row, jnp.max(dt, axis=0, keepdims=True))
        w_inter = jnp.exp2(a_row - m_t)
        st = kq[:L] * jnp.exp2(dt - m_t)
        r = _dot(vt_c, st.astype(BF16)) + kq[L:] * w_inter
        den = r[MLSTM_V_DIM:MLSTM_V_DIM + 1]
        scale = 1.0 / jnp.maximum(jnp.abs(den), jnp.exp2(-m_t))
        ht_s[h % MLSTM_HPG, :, rows] += r[:MLSTM_V_DIM] * scale
        g_row = b_tot - b_row + i_row
        m_new = jnp.maximum(b_tot + m, jnp.max(g_row, axis=-1, keepdims=True))
        decay = jnp.exp2(b_tot + m - m_new)
        wk = jnp.exp2(g_row - m_new)
        vw = (vt_c.astype(F32) * wk).astype(BF16)
        st_s[sidx] = decay * state + _dot(vw, k_c)
        m_s[sidx] = m_new

    for hg in range(nh // MLSTM_HPG):
        heads = range(hg * MLSTM_HPG, (hg + 1) * MLSTM_HPG)
        for h in heads:
            for d in range(2):
                st_s[d * nh + h] = jnp.zeros((MLSTM_VA, MLSTM_QK_DIM), F32)
                m_s[d * nh + h] = jnp.zeros((1, 1), F32)
        for h in heads:
            ht_s[h % MLSTM_HPG] = jnp.zeros((MLSTM_V_DIM, s), F32)

        def body(ci, carry, heads=heads):
            chains = [(h, d) for h in heads for d in range(2)]
            staged = [chunk_scores(h, d, ci) for h, d in chains]
            for (h, d), args in zip(chains, staged):
                chunk_finish(h, d, *args)
            return carry

        lax.fori_loop(0, nc, body, 0)

        for h in heads:
            cols = slice(h * MLSTM_V_DIM, (h + 1) * MLSTM_V_DIM)

            def fin(t, carry, h=h, cols=cols):
                r0 = pl.multiple_of(t * CONV_ROWS, CONV_ROWS)
                rows = pl.ds(r0, CONV_ROWS)
                hst = ht_s[h % MLSTM_HPG, :, rows]
                hn = (hst * lax.rsqrt(jnp.mean(hst * hst, axis=0, keepdims=True) + EPS)).T
                og = _sigmoid(o_ref[0, rows, cols].astype(F32))
                y_ref[0, rows, cols] = (og * (hn * gain_ref[:, cols])).astype(BF16)
                return carry

            lax.fori_loop(0, s // CONV_ROWS, fin, 0)


def _mlstm_call(z, gt, conv_w, conv_b, head_gain):
    b, s, _ = z.shape
    vw = MLSTM_V_WIDTH
    once = pl.Buffered(1)
    return pl.pallas_call(
        _mlstm_kernel,
        out_shape=jax.ShapeDtypeStruct((b, s, vw), BF16),
        grid=(b,),
        in_specs=[
            pl.BlockSpec((1, s, 2 * MLSTM_QK_WIDTH), lambda i: (i, 0, 0), pipeline_mode=once),
            pl.BlockSpec((1, s, vw), lambda i: (i, 0, 1), pipeline_mode=once),
            pl.BlockSpec((1, s, vw), lambda i: (i, 0, 2), pipeline_mode=once),
            pl.BlockSpec((1, N_GATES, s), lambda i: (i, 0, 0)),
            pl.BlockSpec(conv_w.shape, lambda i: (0, 0)),
            pl.BlockSpec(conv_b.shape, lambda i: (0, 0)),
            pl.BlockSpec(head_gain.shape, lambda i: (0, 0)),
        ],
        out_specs=pl.BlockSpec((1, s, vw), lambda i: (i, 0, 0)),
        scratch_shapes=[
            pltpu.VMEM((s + 2 * CONV_PAD, LANES), F32),
            pltpu.VMEM((MLSTM_HEADS, s, MLSTM_QK_DIM), BF16),
            pltpu.VMEM((MLSTM_HEADS, s, MLSTM_QK_DIM), BF16),
            pltpu.VMEM((MLSTM_HEADS, MLSTM_VA, s), BF16),
            pltpu.VMEM((s, 2 * MLSTM_HEADS), F32),
            pltpu.VMEM((N_GATES, s), F32),
            pltpu.VMEM((2 * MLSTM_HEADS, MLSTM_VA, MLSTM_QK_DIM), F32),
            pltpu.VMEM((2 * MLSTM_HEADS, 1, 1), F32),
            pltpu.VMEM((MLSTM_HPG, MLSTM_V_DIM, s), F32),
        ],
        compiler_params=_cparams("parallel"),
        name="mlstm_mixer",
    )(z, z, z, gt, conv_w, conv_b, head_gain)


FFN_SLABS = ((0, 1280), (1280, 2816))


def _post_kernel(x_ref, y_ref, mod_ref, wo_ref, w1_ref, w3_ref, w2_ref, o_ref):
    m = mod_ref[0]
    g1, sh2, sc2, g2 = m[2:3], m[3:4], m[4:5], m[5:6]
    x1 = x_ref[0] + g1 * _dot(y_ref[0], wo_ref[...])
    h = _norm_mod(x1, sh2, sc2).astype(BF16)
    acc = None
    for lo, hi in FFN_SLABS:
        a = _dot(h, w1_ref[:, lo:hi])
        b = _dot(h, w3_ref[:, lo:hi])
        gact = (_silu(a) * b).astype(BF16)
        contrib = _dot(gact, w2_ref[lo:hi, :])
        acc = contrib if acc is None else acc + contrib
    o_ref[0] = x1 + g2 * acc


def _post_call(x, y, mod, w_out, w1, w3, w2):
    b, s, d = x.shape
    tm = min(512, s)
    const2 = lambda i, j: (0, 0)
    assert FFN_SLABS[-1][1] == w1.shape[1]
    return pl.pallas_call(
        _post_kernel,
        out_shape=jax.ShapeDtypeStruct((b, s, d), F32),
        grid=(b, s // tm),
        in_specs=[
            pl.BlockSpec((1, tm, d), lambda i, j: (i, j, 0)),
            pl.BlockSpec((1, tm, y.shape[2]), lambda i, j: (i, j, 0)),
            pl.BlockSpec((1, 6, d), lambda i, j: (i, 0, 0)),
            pl.BlockSpec(w_out.shape, lambda i, j: (0, 0)),
            pl.BlockSpec(w1.shape, const2, pipeline_mode=pl.Buffered(1)),
            pl.BlockSpec(w3.shape, const2, pipeline_mode=pl.Buffered(1)),
            pl.BlockSpec(w2.shape, const2, pipeline_mode=pl.Buffered(1)),
        ],
        out_specs=pl.BlockSpec((1, tm, d), lambda i, j: (i, j, 0)),
        compiler_params=_cparams("parallel", "parallel"),
        name="post_ffn",
    )(x, y, mod, w_out, w1, w3, w2)


def _rope_tables(s):
    rows = s // GRID_W
    row = jnp.repeat(jnp.arange(rows), GRID_W).astype(F32)
    col = jnp.tile(jnp.arange(GRID_W), rows).astype(F32)
    n_freq = ATT_HEAD_DIM // 4
    inv_freq = ROPE_THETA ** (-jnp.arange(n_freq, dtype=F32) / n_freq)
    ang = jnp.concatenate([row[:, None] * inv_freq, col[:, None] * inv_freq], axis=-1)
    cos, sin = jnp.cos(ang), jnp.sin(ang)
    cos_t = jnp.tile(cos, (1, LANES // cos.shape[1]))
    sin_t = jnp.tile(jnp.concatenate([-sin, sin], axis=-1), (1, LANES // ATT_HEAD_DIM))
    return cos_t, sin_t


def _pool_band():
    r = np.arange(POOL_TILE)[:, None]
    c = np.arange(POOL_TILE + 2 * POOL_HALO)[None, :]
    bands = []
    for w in POOL_WINDOWS:
        off = c - POOL_HALO - r + w // 2
        bands.append((off >= 0) & (off < w))
    return jnp.asarray(np.stack(bands), dtype=BF16)


def _head_split_perm():
    return np.concatenate([np.arange(0, ATT_HEAD_DIM, 2), np.arange(1, ATT_HEAD_DIM, 2)])


def kernel(x, c, ada_w, ada_b, ev_w_in, ev_w_pool, ev_b_pool, ev_pool_scale, ev_q_gain, ev_k_gain,
           ev_w_out, od_w_in, od_conv_w, od_conv_b, od_gate_b, od_head_gain, od_w_out,
           ffn_w1, ffn_w3, ffn_w2):
    b, s, d = x.shape
    depth = ada_w.shape[0]
    mod = _ada_call(c, ada_w, ada_b).reshape(depth, b, 6, d)
    cos_t, sin_t = _rope_tables(s)
    band = _pool_band()
    perm = _head_split_perm()
    qk_cols = np.concatenate([POOL_DIM + hh * ATT_HEAD_DIM + perm for hh in range(ATT_HEADS + ATT_KV_HEADS)])
    col_perm = np.concatenate([np.arange(POOL_DIM), qk_cols, np.arange(POOL_DIM + ATT_Q_DIM + ATT_KV_DIM,
                                                                      POOL_DIM + ATT_Q_DIM + 2 * ATT_KV_DIM)])

    for layer in range(depth):
        j = layer // 2
        if layer % 2 == 0:
            w_in = ev_w_in[j][:, col_perm].astype(BF16)
            z = _inproj_even_call(x, mod[layer], w_in)
            qg = jnp.tile(ev_q_gain[j][perm], LANES // ATT_HEAD_DIM).reshape(1, LANES)
            kg = jnp.tile(ev_k_gain[j][perm], LANES // ATT_HEAD_DIM).reshape(1, LANES)
            y = _even_mixer_call(z, cos_t, sin_t, qg, kg, band, ev_w_pool[j].astype(BF16),
                                 ev_b_pool[j].reshape(-1, 1, POOL_GROUP_DIM),
                                 ev_pool_scale[j].reshape(-1, 1, POOL_GROUP_DIM))
            w_out = ev_w_out[j].astype(BF16)
        else:
            n_main = 2 * MLSTM_QK_WIDTH + 2 * MLSTM_V_WIDTH
            w_in = od_w_in[j][:, :n_main].astype(BF16)
            wg = od_w_in[j][:, n_main:].astype(BF16)
            z, gt = _inproj_odd_call(x, mod[layer], w_in, wg, od_gate_b[j])
            y = _mlstm_call(z, gt, od_conv_w[j], od_conv_b[j].reshape(1, -1),
                            od_head_gain[j].reshape(1, -1))
            w_out = od_w_out[j].astype(BF16)
        x = _post_call(x, y, mod[layer], w_out, ffn_w1[layer].astype(BF16), ffn_w3[layer].astype(BF16),
                       ffn_w2[layer].astype(BF16))
    return x
```

```python
import jax
import jax.numpy as jnp
import numpy as np
from jax import lax
from jax.experimental import pallas as pl
from jax.experimental.pallas import tpu as pltpu

F32 = jnp.float32
BF16 = jnp.bfloat16

EPS = 1e-6
GRID_W = 64
ROPE_THETA = 10000.0
POOL_WINDOWS = (2, 4, 8, 16)
POOL_GROUP_DIM = 128
POOL_DIM = 512
ATT_HEADS = 8
ATT_KV_HEADS = 2
ATT_GROUP = ATT_HEADS // ATT_KV_HEADS
ATT_HEAD_DIM = 64
ATT_Q_DIM = 512
ATT_KV_DIM = 128
MLSTM_HEADS = 8
MLSTM_QK_DIM = 64
MLSTM_V_DIM = 128
MLSTM_QK_WIDTH = 512
MLSTM_V_WIDTH = 1024
N_GATES = 32
CONV_K = 5

LANES = 128
POOL_HALO = 16
POOL_TILE = 256
CONV_PAD = 8
VMEM_LIMIT = 56 * 1024 * 1024


def _cparams(*sem):
    return pltpu.CompilerParams(dimension_semantics=sem, vmem_limit_bytes=VMEM_LIMIT)


def _sigmoid(x):
    return 1.0 / (1.0 + jnp.exp(-x))


def _silu(x):
    return x * _sigmoid(x)


def _log_sigmoid(x):
    return jnp.minimum(x, 0.0) - jnp.log(1.0 + jnp.exp(-jnp.abs(x)))


def _norm_mod(x, shift, scale):
    ms = jnp.mean(x * x, axis=-1, keepdims=True)
    return (x * lax.rsqrt(ms + EPS)) * (1.0 + scale) + shift


def _split3(x):
    x1 = x.astype(BF16)
    r1 = x - x1.astype(F32)
    x2 = r1.astype(BF16)
    x3 = (r1 - x2.astype(F32)).astype(BF16)
    return x1, x2, x3


def _dot(a, b):
    return jnp.dot(a, b, preferred_element_type=F32)


def _dot_nt(a, b):
    return lax.dot_general(a, b, (((1,), (1,)), ((), ())), preferred_element_type=F32)


def _ada_kernel(c_ref, w_ref, b_ref, o_ref):
    cond = _silu(c_ref[...]).astype(BF16)
    o_ref[0] = _dot(cond, w_ref[0].astype(BF16)) + b_ref[0]


def _ada_call(c, ada_w, ada_b):
    depth, d, n = ada_w.shape
    b = c.shape[0]
    tn = 1024
    return pl.pallas_call(
        _ada_kernel,
        out_shape=jax.ShapeDtypeStruct((depth, b, n), F32),
        grid=(depth, n // tn),
        in_specs=[
            pl.BlockSpec((b, d), lambda l, j: (0, 0)),
            pl.BlockSpec((1, d, tn), lambda l, j: (l, 0, j)),
            pl.BlockSpec((1, 1, tn), lambda l, j: (l, 0, j)),
        ],
        out_specs=pl.BlockSpec((1, b, tn), lambda l, j: (l, 0, j)),
        compiler_params=_cparams("arbitrary", "arbitrary"),
        name="ada_table",
    )(c, ada_w, ada_b.reshape(depth, 1, n))


def _inproj_even_kernel(x_ref, mod_ref, w_ref, u_ref, q_ref, kv_ref):
    m = mod_ref[0]
    h = _norm_mod(x_ref[0], m[0:1], m[1:2]).astype(BF16)
    z = _dot(h, w_ref[...]).astype(BF16)
    u_ref[0] = z[:, :POOL_DIM]
    q_ref[0] = z[:, POOL_DIM:POOL_DIM + ATT_Q_DIM]
    kv_ref[0] = z[:, POOL_DIM + ATT_Q_DIM:]


def _inproj_even_call(x, mod, w):
    b, s, d = x.shape
    n = w.shape[1]
    tm = min(512, s)
    widths = (POOL_DIM, ATT_Q_DIM, 2 * ATT_KV_DIM)
    return pl.pallas_call(
        _inproj_even_kernel,
        out_shape=tuple(jax.ShapeDtypeStruct((b, s, wd), BF16) for wd in widths),
        grid=(b, s // tm),
        in_specs=[
            pl.BlockSpec((1, tm, d), lambda i, j: (i, j, 0)),
            pl.BlockSpec((1, 6, d), lambda i, j: (i, 0, 0)),
            pl.BlockSpec((d, n), lambda i, j: (0, 0)),
        ],
        out_specs=tuple(pl.BlockSpec((1, tm, wd), lambda i, j: (i, j, 0)) for wd in widths),
        compiler_params=_cparams("parallel", "parallel"),
        name="inproj_even",
    )(x, mod, w)


def _inproj_odd_kernel(x_ref, mod_ref, w_ref, wgt_ref, gbt_ref, q_ref, k_ref, v_ref, o_ref, gt_ref):
    m = mod_ref[0]
    h = _norm_mod(x_ref[0], m[0:1], m[1:2]).astype(BF16)
    z = _dot(h, w_ref[...]).astype(BF16)
    q_ref[0] = z[:, :MLSTM_QK_WIDTH]
    k_ref[0] = z[:, MLSTM_QK_WIDTH:2 * MLSTM_QK_WIDTH]
    v_ref[0] = z[:, 2 * MLSTM_QK_WIDTH:2 * MLSTM_QK_WIDTH + MLSTM_V_WIDTH]
    o_ref[0] = z[:, 2 * MLSTM_QK_WIDTH + MLSTM_V_WIDTH:]
    gt_ref[0] = _dot_nt(wgt_ref[...], h) + gbt_ref[...]


def _inproj_odd_call(x, mod, w, wgt, gate_b):
    b, s, d = x.shape
    n = w.shape[1]
    tm = min(512, s)
    widths = (MLSTM_QK_WIDTH, MLSTM_QK_WIDTH, MLSTM_V_WIDTH, MLSTM_V_WIDTH)
    return pl.pallas_call(
        _inproj_odd_kernel,
        out_shape=tuple(jax.ShapeDtypeStruct((b, s, wd), BF16) for wd in widths)
        + (jax.ShapeDtypeStruct((b, N_GATES, s), F32),),
        grid=(b, s // tm),
        in_specs=[
            pl.BlockSpec((1, tm, d), lambda i, j: (i, j, 0)),
            pl.BlockSpec((1, 6, d), lambda i, j: (i, 0, 0)),
            pl.BlockSpec((d, n), lambda i, j: (0, 0)),
            pl.BlockSpec((N_GATES, d), lambda i, j: (0, 0)),
            pl.BlockSpec((N_GATES, 1), lambda i, j: (0, 0)),
        ],
        out_specs=tuple(pl.BlockSpec((1, tm, wd), lambda i, j: (i, j, 0)) for wd in widths)
        + (pl.BlockSpec((1, N_GATES, tm), lambda i, j: (i, 0, j)),),
        compiler_params=_cparams("parallel", "parallel"),
        name="inproj_odd",
    )(x, mod, w, wgt, gate_b.reshape(N_GATES, 1))


ATT_QB = 256
ATT_PREP = 256
LOG2E = 1.4426950408889634
ATT_UNROLL = 4
ATT_VA = ATT_HEAD_DIM + 16


def _seg_mean_sq(x, bd):
    x2 = x * x
    hi = x2.astype(BF16)
    lo = (x2 - hi.astype(F32)).astype(BF16)
    return (_dot(hi, bd) + _dot(lo, bd)) * (1.0 / ATT_HEAD_DIM)


def _rope_swap(x, first_half):
    return jnp.where(first_half, pltpu.roll(x, LANES - 32, 1), pltpu.roll(x, 32, 1))


def _even_mixer_kernel(u_ref, q_ref, kv_ref, cos_ref, sin_ref, qg_ref, kg_ref, band_ref,
                       wp_ref, bp_ref, ps_ref, y_ref, up_s, qs_s, ks_s, vt_s, st_s, mx_s):
    s = u_ref.shape[1]
    lane = lax.broadcasted_iota(jnp.int32, (1, LANES), 1)
    first_half = (lane % ATT_HEAD_DIM) < (ATT_HEAD_DIM // 2)
    r_i = lax.broadcasted_iota(jnp.int32, (LANES, LANES), 0)
    c_i = lax.broadcasted_iota(jnp.int32, (LANES, LANES), 1)
    bd = jnp.where((r_i // ATT_HEAD_DIM) == (c_i // ATT_HEAD_DIM), 1.0, 0.0).astype(BF16)

    def prep(t, carry):
        r0 = pl.multiple_of(t * ATT_PREP, ATT_PREP)
        rows = pl.ds(r0, ATT_PREP)
        cos = cos_ref[rows, :]
        sin = sin_ref[rows, :]

        def norm_rope(xb, gain):
            xb = xb.astype(F32)
            xn = xb * lax.rsqrt(_seg_mean_sq(xb, bd) + EPS) * gain
            return xn * cos + _rope_swap(xn, first_half) * sin

        for cb in range(ATT_Q_DIM // LANES):
            qn = norm_rope(q_ref[0, rows, cb * LANES:(cb + 1) * LANES], qg_ref[...])
            qn = (qn * (ATT_HEAD_DIM ** -0.5 * LOG2E)).astype(BF16)
            qs_s[2 * cb, rows, :] = qn[:, :ATT_HEAD_DIM]
            qs_s[2 * cb + 1, rows, :] = qn[:, ATT_HEAD_DIM:]
        kn = norm_rope(kv_ref[0, rows, 0:LANES], kg_ref[...]).astype(BF16)
        ks_s[0, rows, :] = kn[:, :ATT_HEAD_DIM]
        ks_s[1, rows, :] = kn[:, ATT_HEAD_DIM:]
        vt = kv_ref[0, rows, LANES:2 * LANES].astype(F32).T.astype(BF16)
        ones = jnp.ones((ATT_VA - ATT_HEAD_DIM, ATT_PREP), BF16)
        vt_s[0, :, rows] = jnp.concatenate([vt[:ATT_HEAD_DIM], ones], axis=0)
        vt_s[1, :, rows] = jnp.concatenate([vt[ATT_HEAD_DIM:], ones], axis=0)
        return carry

    lax.fori_loop(0, s // ATT_PREP, prep, 0)

    n_qb = s // ATT_QB
    n_steps = ATT_KV_HEADS * (ATT_GROUP // 2) * n_qb

    def decode(t):
        j = t // ((ATT_GROUP // 2) * n_qb)
        gp = (t // n_qb) % (ATT_GROUP // 2)
        rows = pl.ds(pl.multiple_of((t % n_qb) * ATT_QB, ATT_QB), ATT_QB)
        return j, ATT_GROUP * j + 2 * gp, rows

    def scores(t, slot):
        j, h0, rows = decode(t)
        qstk = jnp.concatenate([qs_s[h0, rows, :], qs_s[h0 + 1, rows, :]], axis=0)
        st = _dot_nt(ks_s[j], qstk)
        st_s[slot] = st
        mx_s[slot] = jnp.max(st, axis=0, keepdims=True)

    def weighted_values(t, slot):
        j, h0, rows = decode(t)
        p = jnp.exp2(st_s[slot] - mx_s[slot]).astype(BF16)
        outs = []
        for g in range(2):
            oa = _dot(vt_s[j], p[:, g * ATT_QB:(g + 1) * ATT_QB])
            outs.append(oa[:ATT_HEAD_DIM] / oa[ATT_HEAD_DIM:ATT_HEAD_DIM + 1])
        pair = jnp.concatenate(outs, axis=0).T
        c0 = pl.multiple_of(POOL_DIM + h0 * ATT_HEAD_DIM, LANES)
        y_ref[0, rows, pl.ds(c0, LANES)] = pair.astype(BF16)

    scores(0, 0)

    def att_body(k, carry):
        t0 = ATT_UNROLL * k
        for u in range(ATT_UNROLL):
            scores(jnp.minimum(t0 + u + 1, n_steps - 1), (u + 1) % 2)
            weighted_values(t0 + u, u % 2)
        return carry

    lax.fori_loop(0, n_steps // ATT_UNROLL, att_body, 0)

    zeros_h = jnp.zeros((POOL_HALO, POOL_DIM), BF16)
    up_s[0:POOL_HALO, :] = zeros_h
    up_s[POOL_HALO + s:2 * POOL_HALO + s, :] = zeros_h
    up_s[POOL_HALO:POOL_HALO + s, :] = u_ref[0]

    def pool_step(t, carry):
        r0 = pl.multiple_of(t * POOL_TILE, POOL_TILE)
        tok = r0 + lax.broadcasted_iota(jnp.int32, (POOL_TILE, 1), 0)
        for g, w in enumerate(POOL_WINDOWS):
            cols = slice(g * POOL_GROUP_DIM, (g + 1) * POOL_GROUP_DIM)
            slab = up_s[pl.ds(r0, POOL_TILE + 2 * POOL_HALO), cols]
            tot = _dot(band_ref[g], slab)
            lo = jnp.maximum(tok - w // 2, 0)
            hi = jnp.minimum(tok + (w - w // 2), s)
            cnt = (hi - lo).astype(F32)
            ug = u_ref[0, pl.ds(r0, POOL_TILE), cols].astype(F32)
            pooled = (tot / cnt - ug).astype(BF16)
            a = (_dot(pooled, wp_ref[g]) + bp_ref[g]) * ps_ref[g]
            y_ref[0, pl.ds(r0, POOL_TILE), cols] = a.astype(BF16)
        return carry

    lax.fori_loop(0, s // POOL_TILE, pool_step, 0)


def _even_mixer_call(u, q, kv, cos_t, sin_t, q_gain, k_gain, band, w_pool, b_pool, pool_scale):
    b, s, _ = u.shape
    return pl.pallas_call(
        _even_mixer_kernel,
        out_shape=jax.ShapeDtypeStruct((b, s, POOL_DIM + ATT_Q_DIM), BF16),
        grid=(b,),
        in_specs=[
            pl.BlockSpec((1, s, POOL_DIM), lambda i: (i, 0, 0)),
            pl.BlockSpec((1, s, ATT_Q_DIM), lambda i: (i, 0, 0)),
            pl.BlockSpec((1, s, 2 * ATT_KV_DIM), lambda i: (i, 0, 0)),
            pl.BlockSpec((s, LANES), lambda i: (0, 0)),
            pl.BlockSpec((s, LANES), lambda i: (0, 0)),
            pl.BlockSpec((1, LANES), lambda i: (0, 0)),
            pl.BlockSpec((1, LANES), lambda i: (0, 0)),
            pl.BlockSpec(band.shape, lambda i: (0, 0, 0)),
            pl.BlockSpec(w_pool.shape, lambda i: (0, 0, 0)),
            pl.BlockSpec(b_pool.shape, lambda i: (0, 0, 0)),
            pl.BlockSpec(pool_scale.shape, lambda i: (0, 0, 0)),
        ],
        out_specs=pl.BlockSpec((1, s, POOL_DIM + ATT_Q_DIM), lambda i: (i, 0, 0)),
        scratch_shapes=[
            pltpu.VMEM((s + 2 * POOL_HALO, POOL_DIM), BF16),
            pltpu.VMEM((ATT_HEADS, s, ATT_HEAD_DIM), BF16),
            pltpu.VMEM((ATT_KV_HEADS, s, ATT_HEAD_DIM), BF16),
            pltpu.VMEM((ATT_KV_HEADS, ATT_VA, s), BF16),
            pltpu.VMEM((2, s, 2 * ATT_QB), F32),
            pltpu.VMEM((2, 1, 2 * ATT_QB), F32),
        ],
        compiler_params=_cparams("parallel"),
        name="even_mixer",
    )(u, q, kv, cos_t, sin_t, q_gain, k_gain, band, w_pool, b_pool, pool_scale)


CONV_ROWS = 256
MLSTM_HPG = 4
MLSTM_VA = MLSTM_V_DIM + 16
MLSTM_CHUNK = 256
GATES_PG = 4 * MLSTM_HPG


def _mlstm_kernel(q_ref, k_ref, v_ref, o_ref, gt_ref, cwq_ref, cwk_ref, cbq_ref, cbk_ref, gain_ref, y_ref,
                  pad_s, qs_s, ks_s, vt_s, ibcol_s, brow_s, st_s, m_s, ht_s):
    s = q_ref.shape[1]
    L = MLSTM_CHUNK
    nc = s // L
    nh = MLSTM_HPG
    r_i = lax.broadcasted_iota(jnp.int32, (L, L), 0)
    c_i = lax.broadcasted_iota(jnp.int32, (L, L), 1)
    low_b = r_i >= c_i
    upp_b = c_i >= r_i
    low01 = jnp.where(low_b, 1.0, 0.0).astype(BF16)
    upp01 = jnp.where(upp_b, 1.0, 0.0).astype(BF16)
    eye01 = jnp.where(r_i == c_i, 1.0, 0.0).astype(BF16)

    def stack(x):
        return jnp.concatenate([x[:, c * L:(c + 1) * L] for c in range(nc)], axis=0)

    gt = gt_ref[0, 0]
    lf3 = jnp.concatenate(_split3(stack(_log_sigmoid(gt) * LOG2E)), axis=0)
    n_st = nc * GATES_PG
    pre = _dot(lf3, upp01)
    suf = _dot(lf3, low01)
    pre = pre[:n_st] + pre[n_st:2 * n_st] + pre[2 * n_st:]
    suf = suf[:n_st] + suf[n_st:2 * n_st] + suf[2 * n_st:]
    fwd_row = (lax.broadcasted_iota(jnp.int32, (n_st, 1), 0) % GATES_PG) < GATES_PG // 2
    bq = jnp.where(fwd_row, pre, suf)
    i2 = stack(gt * LOG2E)
    ib = []
    for c in range(nc):
        r0 = c * GATES_PG
        brow_s[:, c * L:(c + 1) * L] = bq[r0:r0 + GATES_PG]
        ib.append(i2[r0:r0 + nh] - bq[r0 + nh:r0 + 2 * nh])
        ib.append(i2[r0 + 2 * nh:r0 + 3 * nh] - bq[r0 + 3 * nh:r0 + 4 * nh])
    ib3 = jnp.concatenate(_split3(jnp.concatenate(ib, axis=0)), axis=0)
    ibt = _dot_nt(eye01, ib3)
    n_ib = nc * 2 * nh
    ibcol_s[...] = ibt[:, :n_ib] + ibt[:, n_ib:2 * n_ib] + ibt[:, 2 * n_ib:]

    zpad = jnp.zeros((CONV_PAD, LANES), F32)
    pad_s[0:CONV_PAD, :] = zpad
    pad_s[CONV_PAD + s:2 * CONV_PAD + s, :] = zpad
    for src_ref, cw_ref, cb_ref, dst_s, gain in ((q_ref, cwq_ref, cbq_ref, qs_s, 1.0),
                                                 (k_ref, cwk_ref, cbk_ref, ks_s, MLSTM_QK_DIM ** -0.5)):
        for cb in range(nh * MLSTM_QK_DIM // LANES):
            cols = slice(cb * LANES, (cb + 1) * LANES)
            pad_s[CONV_PAD:CONV_PAD + s, :] = src_ref[0, :, cols].astype(F32)
            cw = cw_ref[:, cols]
            cbias = cb_ref[:, cols]
            for t in range(s // CONV_ROWS):
                r0 = t * CONV_ROWS
                acc = cbias + cw[0:1] * pad_s[r0 + CONV_PAD - 2:r0 + CONV_PAD - 2 + CONV_ROWS, :]
                for j in range(1, CONV_K):
                    off = r0 + CONV_PAD - CONV_K // 2 + j
                    acc = acc + cw[j:j + 1] * pad_s[off:off + CONV_ROWS, :]
                vb = (_silu(acc) * gain).astype(BF16) if gain != 1.0 else _silu(acc).astype(BF16)
                dst_s[2 * cb, r0:r0 + CONV_ROWS, :] = vb[:, :MLSTM_QK_DIM]
                dst_s[2 * cb + 1, r0:r0 + CONV_ROWS, :] = vb[:, MLSTM_QK_DIM:]

    ones_rows = jnp.ones((MLSTM_VA - MLSTM_V_DIM, CONV_ROWS), BF16)

    def vt_step(t, carry):
        r0 = pl.multiple_of(t * CONV_ROWS, CONV_ROWS)
        rows = pl.ds(r0, CONV_ROWS)
        for h in range(nh):
            vv = v_ref[0, rows, h * MLSTM_V_DIM:(h + 1) * MLSTM_V_DIM]
            vt_s[h, 0:MLSTM_V_DIM, rows] = vv.astype(F32).T.astype(BF16)
            vt_s[h, MLSTM_V_DIM:MLSTM_VA, rows] = ones_rows
        return carry

    lax.fori_loop(0, s // CONV_ROWS, vt_step, 0)

    def chunk_scores(h, d, ci):
        c = ci if d == 0 else nc - 1 - ci
        rows = slice(c * L, (c + 1) * L)
        sidx = d * nh + h
        state = st_s[sidx]
        q_c = qs_s[h, rows, :]
        k_c = ks_s[h, rows, :]
        kq = _dot_nt(jnp.concatenate([k_c, state.astype(BF16)], axis=0), q_c)
        return rows, state, k_c, kq

    def chunk_finish(h, d, ci, rows, state, k_c, kq):
        c = ci if d == 0 else nc - 1 - ci
        sidx = d * nh + h
        jf = d * 2 * nh + nh + h
        ji = d * 2 * nh + h
        m = m_s[sidx]
        vt_c = vt_s[h, :, rows]
        ib_col = ibcol_s[:, c * 2 * nh + sidx:c * 2 * nh + sidx + 1]
        b_row = brow_s[jf:jf + 1, rows]
        i_row = gt_ref[0, 0, ji:ji + 1, rows] * LOG2E
        b_tot = b_row[:, L - 1:L] if d == 0 else b_row[:, 0:1]
        a_row = b_row + m
        dt = jnp.where(upp_b if d == 0 else low_b, b_row + ib_col, -jnp.inf)
        m_t = jnp.maximum(a_row, jnp.max(dt, axis=0, keepdims=True))
        w_inter = jnp.exp2(a_row - m_t)
        st = kq[:L] * jnp.exp2(dt - m_t)
        r = _dot(vt_c, st.astype(BF16)) + kq[L:] * w_inter
        den = r[MLSTM_V_DIM:MLSTM_V_DIM + 1]
        scale = 1.0 / jnp.maximum(jnp.abs(den), jnp.exp2(-m_t))
        ht_s[h, :, rows] += r[:MLSTM_V_DIM] * scale
        g_row = b_tot - b_row + i_row
        m_new = jnp.maximum(b_tot + m, jnp.max(g_row, axis=-1, keepdims=True))
        decay = jnp.exp2(b_tot + m - m_new)
        wk = jnp.exp2(g_row - m_new)
        vw = (vt_c.astype(F32) * wk).astype(BF16)
        st_s[sidx] = decay * state + _dot(vw, k_c)
        m_s[sidx] = m_new

    chains = [(h, d) for h in range(nh) for d in range(2)]
    for h, d in chains:
        st_s[d * nh + h] = jnp.zeros((MLSTM_VA, MLSTM_QK_DIM), F32)
        m_s[d * nh + h] = jnp.zeros((1, 1), F32)
    for h in range(nh):
        ht_s[h] = jnp.zeros((MLSTM_V_DIM, s), F32)

    for ci in range(nc):
        staged = [chunk_scores(h, d, ci) for h, d in chains]
        for (h, d), args in zip(chains, staged):
            chunk_finish(h, d, ci, *args)

    for h in range(nh):
        cols = slice(h * MLSTM_V_DIM, (h + 1) * MLSTM_V_DIM)

        def fin(t, carry, h=h, cols=cols):
            r0 = pl.multiple_of(t * CONV_ROWS, CONV_ROWS)
            rows = pl.ds(r0, CONV_ROWS)
            hst = ht_s[h, :, rows]
            hn = (hst * lax.rsqrt(jnp.mean(hst * hst, axis=0, keepdims=True) + EPS)).T
            og = _sigmoid(o_ref[0, rows, cols].astype(F32))
            y_ref[0, rows, cols] = (og * (hn * gain_ref[:, cols])).astype(BF16)
            return carry

        lax.fori_loop(0, s // CONV_ROWS, fin, 0)


def _mlstm_call(q, k, v, o, gt, conv_w, conv_b, head_gain):
    b, s, _ = q.shape
    n_hg = MLSTM_HEADS // MLSTM_HPG
    qw = MLSTM_HPG * MLSTM_QK_DIM
    vw = MLSTM_HPG * MLSTM_V_DIM
    gt = gt.reshape(b, n_hg, GATES_PG, s)
    return pl.pallas_call(
        _mlstm_kernel,
        out_shape=jax.ShapeDtypeStruct((b, s, MLSTM_V_WIDTH), BF16),
        grid=(b, n_hg),
        in_specs=[
            pl.BlockSpec((1, s, qw), lambda i, g: (i, 0, g)),
            pl.BlockSpec((1, s, qw), lambda i, g: (i, 0, g)),
            pl.BlockSpec((1, s, vw), lambda i, g: (i, 0, g)),
            pl.BlockSpec((1, s, vw), lambda i, g: (i, 0, g)),
            pl.BlockSpec((1, 1, GATES_PG, s), lambda i, g: (i, g, 0, 0)),
            pl.BlockSpec((CONV_K, qw), lambda i, g: (0, g)),
            pl.BlockSpec((CONV_K, qw), lambda i, g: (0, n_hg + g)),
            pl.BlockSpec((1, qw), lambda i, g: (0, g)),
            pl.BlockSpec((1, qw), lambda i, g: (0, n_hg + g)),
            pl.BlockSpec((1, vw), lambda i, g: (0, g)),
        ],
        out_specs=pl.BlockSpec((1, s, vw), lambda i, g: (i, 0, g)),
        scratch_shapes=[
            pltpu.VMEM((s + 2 * CONV_PAD, LANES), F32),
            pltpu.VMEM((MLSTM_HPG, s, MLSTM_QK_DIM), BF16),
            pltpu.VMEM((MLSTM_HPG, s, MLSTM_QK_DIM), BF16),
            pltpu.VMEM((MLSTM_HPG, MLSTM_VA, s), BF16),
            pltpu.VMEM((MLSTM_CHUNK, (s // MLSTM_CHUNK) * 2 * MLSTM_HPG), F32),
            pltpu.VMEM((GATES_PG, s), F32),
            pltpu.VMEM((2 * MLSTM_HPG, MLSTM_VA, MLSTM_QK_DIM), F32),
            pltpu.VMEM((2 * MLSTM_HPG, 1, 1), F32),
            pltpu.VMEM((MLSTM_HPG, MLSTM_V_DIM, s), F32),
        ],
        compiler_params=_cparams("parallel", "parallel"),
        name="mlstm_mixer",
    )(q, k, v, o, gt, conv_w, conv_w, conv_b, conv_b, head_gain)


FFN_SLABS = ((0, 1280), (1280, 2816))


def _post_kernel(x_ref, y_ref, mod_ref, wo_ref, w1_ref, w3_ref, w2_ref, o_ref):
    m = mod_ref[0]
    g1, sh2, sc2, g2 = m[2:3], m[3:4], m[4:5], m[5:6]
    x1 = x_ref[0] + g1 * _dot(y_ref[0], wo_ref[...])
    h = _norm_mod(x1, sh2, sc2).astype(BF16)
    acc = None
    for lo, hi in FFN_SLABS:
        a = _dot(h, w1_ref[:, lo:hi])
        b = _dot(h, w3_ref[:, lo:hi])
        gact = (_silu(a) * b).astype(BF16)
        contrib = _dot(gact, w2_ref[lo:hi, :])
        acc = contrib if acc is None else acc + contrib
    o_ref[0] = x1 + g2 * acc


def _post_call(x, y, mod, w_out, w1, w3, w2):
    b, s, d = x.shape
    tm = min(512, s)
    const2 = lambda i, j: (0, 0)
    assert FFN_SLABS[-1][1] == w1.shape[1]
    return pl.pallas_call(
        _post_kernel,
        out_shape=jax.ShapeDtypeStruct((b, s, d), F32),
        grid=(b, s // tm),
        in_specs=[
            pl.BlockSpec((1, tm, d), lambda i, j: (i, j, 0)),
            pl.BlockSpec((1, tm, y.shape[2]), lambda i, j: (i, j, 0)),
            pl.BlockSpec((1, 6, d), lambda i, j: (i, 0, 0)),
            pl.BlockSpec(w_out.shape, lambda i, j: (0, 0)),
            pl.BlockSpec(w1.shape, const2, pipeline_mode=pl.Buffered(1)),
            pl.BlockSpec(w3.shape, const2, pipeline_mode=pl.Buffered(1)),
            pl.BlockSpec(w2.shape, const2, pipeline_mode=pl.Buffered(1)),
        ],
        out_specs=pl.BlockSpec((1, tm, d), lambda i, j: (i, j, 0)),
        compiler_params=_cparams("parallel", "parallel"),
        name="post_ffn",
    )(x, y, mod, w_out, w1, w3, w2)


def _rope_tables(s):
    rows = s // GRID_W
    row = jnp.repeat(jnp.arange(rows), GRID_W).astype(F32)
    col = jnp.tile(jnp.arange(GRID_W), rows).astype(F32)
    n_freq = ATT_HEAD_DIM // 4
    inv_freq = ROPE_THETA ** (-jnp.arange(n_freq, dtype=F32) / n_freq)
    ang = jnp.concatenate([row[:, None] * inv_freq, col[:, None] * inv_freq], axis=-1)
    cos, sin = jnp.cos(ang), jnp.sin(ang)
    cos_t = jnp.tile(cos, (1, LANES // cos.shape[1]))
    sin_t = jnp.tile(jnp.concatenate([-sin, sin], axis=-1), (1, LANES // ATT_HEAD_DIM))
    return cos_t, sin_t


def _pool_band():
    r = np.arange(POOL_TILE)[:, None]
    c = np.arange(POOL_TILE + 2 * POOL_HALO)[None, :]
    bands = []
    for w in POOL_WINDOWS:
        off = c - POOL_HALO - r + w // 2
        bands.append((off >= 0) & (off < w))
    return jnp.asarray(np.stack(bands), dtype=BF16)


def _head_split_perm():
    return np.concatenate([np.arange(0, ATT_HEAD_DIM, 2), np.arange(1, ATT_HEAD_DIM, 2)])


def kernel(x, c, ada_w, ada_b, ev_w_in, ev_w_pool, ev_b_pool, ev_pool_scale, ev_q_gain, ev_k_gain,
           ev_w_out, od_w_in, od_conv_w, od_conv_b, od_gate_b, od_head_gain, od_w_out,
           ffn_w1, ffn_w3, ffn_w2):
    b, s, d = x.shape
    depth = ada_w.shape[0]
    mod = _ada_call(c, ada_w, ada_b).reshape(depth, b, 6, d)
    cos_t, sin_t = _rope_tables(s)
    band = _pool_band()
    perm = _head_split_perm()
    qk_cols = np.concatenate([POOL_DIM + hh * ATT_HEAD_DIM + perm for hh in range(ATT_HEADS + ATT_KV_HEADS)])
    col_perm = np.concatenate([np.arange(POOL_DIM), qk_cols, np.arange(POOL_DIM + ATT_Q_DIM + ATT_KV_DIM,
                                                                      POOL_DIM + ATT_Q_DIM + 2 * ATT_KV_DIM)])
    gate_perm = np.arange(N_GATES).reshape(2, 2, MLSTM_HEADS // MLSTM_HPG, MLSTM_HPG).transpose(2, 0, 1, 3).reshape(-1)

    for layer in range(depth):
        j = layer // 2
        if layer % 2 == 0:
            w_in = ev_w_in[j][:, col_perm].astype(BF16)
            u, q, kv = _inproj_even_call(x, mod[layer], w_in)
            qg = jnp.tile(ev_q_gain[j][perm], LANES // ATT_HEAD_DIM).reshape(1, LANES)
            kg = jnp.tile(ev_k_gain[j][perm], LANES // ATT_HEAD_DIM).reshape(1, LANES)
            y = _even_mixer_call(u, q, kv, cos_t, sin_t, qg, kg, band, ev_w_pool[j].astype(BF16),
                                 ev_b_pool[j].reshape(-1, 1, POOL_GROUP_DIM),
                                 ev_pool_scale[j].reshape(-1, 1, POOL_GROUP_DIM))
            w_out = ev_w_out[j].astype(BF16)
        else:
            n_main = 2 * MLSTM_QK_WIDTH + 2 * MLSTM_V_WIDTH
            w_in = od_w_in[j][:, :n_main].astype(BF16)
            wgt = od_w_in[j][:, n_main:][:, gate_perm].T.astype(BF16)
            q, k, v, o, gt = _inproj_odd_call(x, mod[layer], w_in, wgt, od_gate_b[j][gate_perm])
            y = _mlstm_call(q, k, v, o, gt, od_conv_w[j], od_conv_b[j].reshape(1, -1),
                            od_head_gain[j].reshape(1, -1))
            w_out = od_w_out[j].astype(BF16)
        x = _post_call(x, y, mod[layer], w_out, ffn_w1[layer].astype(BF16), ffn_w3[layer].astype(BF16),
                       ffn_w2[layer].astype(BF16))
    return x
```

```python
import jax
import jax.numpy as jnp
import numpy as np
from jax import lax
from jax.experimental import pallas as pl
from jax.experimental.pallas import tpu as pltpu

F32 = jnp.float32
BF16 = jnp.bfloat16

EPS = 1e-6
GRID_W = 64
ROPE_THETA = 10000.0
POOL_WINDOWS = (2, 4, 8, 16)
POOL_GROUP_DIM = 128
POOL_DIM = 512
ATT_HEADS = 8
ATT_KV_HEADS = 2
ATT_GROUP = ATT_HEADS // ATT_KV_HEADS
ATT_HEAD_DIM = 64
ATT_Q_DIM = 512
ATT_KV_DIM = 128
MLSTM_HEADS = 8
MLSTM_QK_DIM = 64
MLSTM_V_DIM = 128
MLSTM_QK_WIDTH = 512
MLSTM_V_WIDTH = 1024
N_GATES = 32
CONV_K = 5

LANES = 128
POOL_HALO = 16
POOL_TILE = 256
CONV_PAD = 8
VMEM_LIMIT = 56 * 1024 * 1024


def _cparams(*sem):
    return pltpu.CompilerParams(dimension_semantics=sem, vmem_limit_bytes=VMEM_LIMIT)


def _sigmoid(x):
    return 1.0 / (1.0 + jnp.exp(-x))


def _silu(x):
    return x * _sigmoid(x)


def _log_sigmoid(x):
    return jnp.minimum(x, 0.0) - jnp.log(1.0 + jnp.exp(-jnp.abs(x)))


def _norm_mod(x, shift, scale):
    ms = jnp.mean(x * x, axis=-1, keepdims=True)
    return (x * lax.rsqrt(ms + EPS)) * (1.0 + scale) + shift


def _split3(x):
    x1 = x.astype(BF16)
    r1 = x - x1.astype(F32)
    x2 = r1.astype(BF16)
    x3 = (r1 - x2.astype(F32)).astype(BF16)
    return x1, x2, x3


def _dot(a, b):
    return jnp.dot(a, b, preferred_element_type=F32)


def _dot_nt(a, b):
    return lax.dot_general(a, b, (((1,), (1,)), ((), ())), preferred_element_type=F32)


def _ada_kernel(c_ref, w_ref, b_ref, o_ref):
    cond = _silu(c_ref[...]).astype(BF16)
    o_ref[0] = _dot(cond, w_ref[0].astype(BF16)) + b_ref[0]


def _ada_call(c, ada_w, ada_b):
    depth, d, n = ada_w.shape
    b = c.shape[0]
    tn = 1024
    return pl.pallas_call(
        _ada_kernel,
        out_shape=jax.ShapeDtypeStruct((depth, b, n), F32),
        grid=(depth, n // tn),
        in_specs=[
            pl.BlockSpec((b, d), lambda l, j: (0, 0)),
            pl.BlockSpec((1, d, tn), lambda l, j: (l, 0, j)),
            pl.BlockSpec((1, 1, tn), lambda l, j: (l, 0, j)),
        ],
        out_specs=pl.BlockSpec((1, b, tn), lambda l, j: (l, 0, j)),
        compiler_params=_cparams("arbitrary", "arbitrary"),
        name="ada_table",
    )(c, ada_w, ada_b.reshape(depth, 1, n))


def _inproj_even_kernel(x_ref, mod_ref, w_ref, u_ref, q_ref, kv_ref):
    m = mod_ref[0]
    h = _norm_mod(x_ref[0], m[0:1], m[1:2]).astype(BF16)
    z = _dot(h, w_ref[...]).astype(BF16)
    u_ref[0] = z[:, :POOL_DIM]
    q_ref[0] = z[:, POOL_DIM:POOL_DIM + ATT_Q_DIM]
    kv_ref[0] = z[:, POOL_DIM + ATT_Q_DIM:]


def _inproj_even_call(x, mod, w):
    b, s, d = x.shape
    n = w.shape[1]
    tm = min(512, s)
    widths = (POOL_DIM, ATT_Q_DIM, 2 * ATT_KV_DIM)
    return pl.pallas_call(
        _inproj_even_kernel,
        out_shape=tuple(jax.ShapeDtypeStruct((b, s, wd), BF16) for wd in widths),
        grid=(b, s // tm),
        in_specs=[
            pl.BlockSpec((1, tm, d), lambda i, j: (i, j, 0)),
            pl.BlockSpec((1, 6, d), lambda i, j: (i, 0, 0)),
            pl.BlockSpec((d, n), lambda i, j: (0, 0)),
        ],
        out_specs=tuple(pl.BlockSpec((1, tm, wd), lambda i, j: (i, j, 0)) for wd in widths),
        compiler_params=_cparams("parallel", "parallel"),
        name="inproj_even",
    )(x, mod, w)


def _inproj_odd_kernel(x_ref, mod_ref, w_ref, wgt_ref, gbt_ref, q_ref, k_ref, v_ref, o_ref, gt_ref):
    m = mod_ref[0]
    h = _norm_mod(x_ref[0], m[0:1], m[1:2]).astype(BF16)
    z = _dot(h, w_ref[...]).astype(BF16)
    q_ref[0] = z[:, :MLSTM_QK_WIDTH]
    k_ref[0] = z[:, MLSTM_QK_WIDTH:2 * MLSTM_QK_WIDTH]
    v_ref[0] = z[:, 2 * MLSTM_QK_WIDTH:2 * MLSTM_QK_WIDTH + MLSTM_V_WIDTH]
    o_ref[0] = z[:, 2 * MLSTM_QK_WIDTH + MLSTM_V_WIDTH:]
    gt_ref[0] = _dot_nt(wgt_ref[...], h) + gbt_ref[...]


def _inproj_odd_call(x, mod, w, wgt, gate_b):
    b, s, d = x.shape
    n = w.shape[1]
    tm = min(512, s)
    widths = (MLSTM_QK_WIDTH, MLSTM_QK_WIDTH, MLSTM_V_WIDTH, MLSTM_V_WIDTH)
    return pl.pallas_call(
        _inproj_odd_kernel,
        out_shape=tuple(jax.ShapeDtypeStruct((b, s, wd), BF16) for wd in widths)
        + (jax.ShapeDtypeStruct((b, N_GATES, s), F32),),
        grid=(b, s // tm),
        in_specs=[
            pl.BlockSpec((1, tm, d), lambda i, j: (i, j, 0)),
            pl.BlockSpec((1, 6, d), lambda i, j: (i, 0, 0)),
            pl.BlockSpec((d, n), lambda i, j: (0, 0)),
            pl.BlockSpec((N_GATES, d), lambda i, j: (0, 0)),
            pl.BlockSpec((N_GATES, 1), lambda i, j: (0, 0)),
        ],
        out_specs=tuple(pl.BlockSpec((1, tm, wd), lambda i, j: (i, j, 0)) for wd in widths)
        + (pl.BlockSpec((1, N_GATES, tm), lambda i, j: (i, 0, j)),),
        compiler_params=_cparams("parallel", "parallel"),
        name="inproj_odd",
    )(x, mod, w, wgt, gate_b.reshape(N_GATES, 1))


ATT_QB = 256
ATT_PREP = 256
LOG2E = 1.4426950408889634
ATT_UNROLL = 8
ATT_KB = 512
ATT_VA = ATT_HEAD_DIM + 16


def _seg_mean_sq(x, bd):
    x2 = x * x
    hi = x2.astype(BF16)
    lo = (x2 - hi.astype(F32)).astype(BF16)
    return (_dot(hi, bd) + _dot(lo, bd)) * (1.0 / ATT_HEAD_DIM)


def _rope_swap(x, first_half):
    return jnp.where(first_half, pltpu.roll(x, LANES - 32, 1), pltpu.roll(x, 32, 1))


def _even_mixer_kernel(u_ref, q_ref, kv_ref, cos_ref, sin_ref, qg_ref, kg_ref, band_ref,
                       wp_ref, bp_ref, ps_ref, y_ref, up_s, qs_s, ks_s, vt_s, st_s, mx_s):
    s = u_ref.shape[1]
    lane = lax.broadcasted_iota(jnp.int32, (1, LANES), 1)
    first_half = (lane % ATT_HEAD_DIM) < (ATT_HEAD_DIM // 2)
    r_i = lax.broadcasted_iota(jnp.int32, (LANES, LANES), 0)
    c_i = lax.broadcasted_iota(jnp.int32, (LANES, LANES), 1)
    bd = jnp.where((r_i // ATT_HEAD_DIM) == (c_i // ATT_HEAD_DIM), 1.0, 0.0).astype(BF16)

    def prep(t, carry):
        r0 = pl.multiple_of(t * ATT_PREP, ATT_PREP)
        rows = pl.ds(r0, ATT_PREP)
        cos = cos_ref[rows, :]
        sin = sin_ref[rows, :]

        n_qc = ATT_Q_DIM // LANES
        xs = [q_ref[0, rows, cb * LANES:(cb + 1) * LANES].astype(F32) for cb in range(n_qc)]
        xs.append(kv_ref[0, rows, 0:LANES].astype(F32))
        gains = [qg_ref[...]] * n_qc + [kg_ref[...]]
        mss = [_seg_mean_sq(xb, bd) for xb in xs]
        roped = []
        for xb, ms, gain in zip(xs, mss, gains):
            xn = xb * lax.rsqrt(ms + EPS) * gain
            roped.append(xn * cos + _rope_swap(xn, first_half) * sin)
        for cb in range(n_qc):
            qn = (roped[cb] * (ATT_HEAD_DIM ** -0.5 * LOG2E)).astype(BF16)
            qs_s[2 * cb, rows, :] = qn[:, :ATT_HEAD_DIM]
            qs_s[2 * cb + 1, rows, :] = qn[:, ATT_HEAD_DIM:]
        kn = roped[n_qc].astype(BF16)
        ks_s[0, rows, :] = kn[:, :ATT_HEAD_DIM]
        ks_s[1, rows, :] = kn[:, ATT_HEAD_DIM:]
        vt = kv_ref[0, rows, LANES:2 * LANES].astype(F32).T.astype(BF16)
        ones = jnp.ones((ATT_VA - ATT_HEAD_DIM, ATT_PREP), BF16)
        vt_s[0, :, rows] = jnp.concatenate([vt[:ATT_HEAD_DIM], ones], axis=0)
        vt_s[1, :, rows] = jnp.concatenate([vt[ATT_HEAD_DIM:], ones], axis=0)
        return carry

    lax.fori_loop(0, s // ATT_PREP, prep, 0)

    n_qb = s // ATT_QB
    n_steps = ATT_KV_HEADS * (ATT_GROUP // 2) * n_qb

    def decode(t):
        j = t // ((ATT_GROUP // 2) * n_qb)
        gp = (t // n_qb) % (ATT_GROUP // 2)
        rows = pl.ds(pl.multiple_of((t % n_qb) * ATT_QB, ATT_QB), ATT_QB)
        return j, ATT_GROUP * j + 2 * gp, rows

    n_kb = s // ATT_KB

    def att_step(t_next, t_cur, slot_next, slot_cur):
        jn, hn, rows_n = decode(t_next)
        jc, hc, rows_c = decode(t_cur)
        qstk = jnp.concatenate([qs_s[hn, rows_n, :], qs_s[hn + 1, rows_n, :]], axis=0)
        mx_cur = mx_s[slot_cur]
        mx_next = None
        acc = [None, None]
        for kb in range(n_kb):
            keys = slice(kb * ATT_KB, (kb + 1) * ATT_KB)
            st = _dot_nt(ks_s[jn, keys, :], qstk)
            st_s[slot_next, keys, :] = st
            bmx = jnp.max(st, axis=0, keepdims=True)
            mx_next = bmx if mx_next is None else jnp.maximum(mx_next, bmx)
            p = jnp.exp2(st_s[slot_cur, keys, :] - mx_cur).astype(BF16)
            for g in range(2):
                part = _dot(vt_s[jc, :, keys], p[:, g * ATT_QB:(g + 1) * ATT_QB])
                acc[g] = part if acc[g] is None else acc[g] + part
        mx_s[slot_next] = mx_next
        outs = [a[:ATT_HEAD_DIM] / a[ATT_HEAD_DIM:ATT_HEAD_DIM + 1] for a in acc]
        pair = jnp.concatenate(outs, axis=0).T
        c0 = pl.multiple_of(POOL_DIM + hc * ATT_HEAD_DIM, LANES)
        y_ref[0, rows_c, pl.ds(c0, LANES)] = pair.astype(BF16)

    j0, h00, rows0 = decode(0)
    st0 = _dot_nt(ks_s[j0], jnp.concatenate([qs_s[h00, rows0, :], qs_s[h00 + 1, rows0, :]], axis=0))
    st_s[0] = st0
    mx_s[0] = jnp.max(st0, axis=0, keepdims=True)

    def att_body(k, carry):
        t0 = ATT_UNROLL * k
        for u in range(ATT_UNROLL):
            att_step(jnp.minimum(t0 + u + 1, n_steps - 1), t0 + u, (u + 1) % 2, u % 2)
        return carry

    lax.fori_loop(0, n_steps // ATT_UNROLL, att_body, 0)

    zeros_h = jnp.zeros((POOL_HALO, POOL_DIM), BF16)
    up_s[0:POOL_HALO, :] = zeros_h
    up_s[POOL_HALO + s:2 * POOL_HALO + s, :] = zeros_h
    up_s[POOL_HALO:POOL_HALO + s, :] = u_ref[0]

    def pool_step(t, carry):
        r0 = pl.multiple_of(t * POOL_TILE, POOL_TILE)
        tok = r0 + lax.broadcasted_iota(jnp.int32, (POOL_TILE, 1), 0)
        groups = list(enumerate(POOL_WINDOWS))
        cols = [slice(g * POOL_GROUP_DIM, (g + 1) * POOL_GROUP_DIM) for g, _ in groups]
        tots = [_dot(band_ref[g], up_s[pl.ds(r0, POOL_TILE + 2 * POOL_HALO), cols[g]]) for g, _ in groups]
        pooled = []
        for g, w in groups:
            lo = jnp.maximum(tok - w // 2, 0)
            hi = jnp.minimum(tok + (w - w // 2), s)
            cnt = (hi - lo).astype(F32)
            ug = u_ref[0, pl.ds(r0, POOL_TILE), cols[g]].astype(F32)
            pooled.append((tots[g] / cnt - ug).astype(BF16))
        proj = [_dot(pooled[g], wp_ref[g]) for g, _ in groups]
        for g, _ in groups:
            y_ref[0, pl.ds(r0, POOL_TILE), cols[g]] = ((proj[g] + bp_ref[g]) * ps_ref[g]).astype(BF16)
        return carry

    lax.fori_loop(0, s // POOL_TILE, pool_step, 0)


def _even_mixer_call(u, q, kv, cos_t, sin_t, q_gain, k_gain, band, w_pool, b_pool, pool_scale):
    b, s, _ = u.shape
    return pl.pallas_call(
        _even_mixer_kernel,
        out_shape=jax.ShapeDtypeStruct((b, s, POOL_DIM + ATT_Q_DIM), BF16),
        grid=(b,),
        in_specs=[
            pl.BlockSpec((1, s, POOL_DIM), lambda i: (i, 0, 0)),
            pl.BlockSpec((1, s, ATT_Q_DIM), lambda i: (i, 0, 0)),
            pl.BlockSpec((1, s, 2 * ATT_KV_DIM), lambda i: (i, 0, 0)),
            pl.BlockSpec((s, LANES), lambda i: (0, 0)),
            pl.BlockSpec((s, LANES), lambda i: (0, 0)),
            pl.BlockSpec((1, LANES), lambda i: (0, 0)),
            pl.BlockSpec((1, LANES), lambda i: (0, 0)),
            pl.BlockSpec(band.shape, lambda i: (0, 0, 0)),
            pl.BlockSpec(w_pool.shape, lambda i: (0, 0, 0)),
            pl.BlockSpec(b_pool.shape, lambda i: (0, 0, 0)),
            pl.BlockSpec(pool_scale.shape, lambda i: (0, 0, 0)),
        ],
        out_specs=pl.BlockSpec((1, s, POOL_DIM + ATT_Q_DIM), lambda i: (i, 0, 0)),
        scratch_shapes=[
            pltpu.VMEM((s + 2 * POOL_HALO, POOL_DIM), BF16),
            pltpu.VMEM((ATT_HEADS, s, ATT_HEAD_DIM), BF16),
            pltpu.VMEM((ATT_KV_HEADS, s, ATT_HEAD_DIM), BF16),
            pltpu.VMEM((ATT_KV_HEADS, ATT_VA, s), BF16),
            pltpu.VMEM((2, s, 2 * ATT_QB), F32),
            pltpu.VMEM((2, 1, 2 * ATT_QB), F32),
        ],
        compiler_params=_cparams("parallel"),
        name="even_mixer",
    )(u, q, kv, cos_t, sin_t, q_gain, k_gain, band, w_pool, b_pool, pool_scale)


CONV_ROWS = 256
MLSTM_HPG = 4
MLSTM_VA = MLSTM_V_DIM + 16
MLSTM_CHUNK = 256
GATES_PG = 4 * MLSTM_HPG


def _mlstm_kernel(q_ref, k_ref, v_ref, o_ref, gt_ref, cwq_ref, cwk_ref, cbq_ref, cbk_ref, gain_ref, y_ref,
                  pad_s, qs_s, ks_s, vt_s, ibcol_s, brow_s, st_s, m_s, ht_s):
    s = q_ref.shape[1]
    L = MLSTM_CHUNK
    nc = s // L
    nh = MLSTM_HPG
    r_i = lax.broadcasted_iota(jnp.int32, (L, L), 0)
    c_i = lax.broadcasted_iota(jnp.int32, (L, L), 1)
    low_b = r_i >= c_i
    upp_b = c_i >= r_i
    low01 = jnp.where(low_b, 1.0, 0.0).astype(BF16)
    upp01 = jnp.where(upp_b, 1.0, 0.0).astype(BF16)
    eye01 = jnp.where(r_i == c_i, 1.0, 0.0).astype(BF16)

    def stack(x):
        return jnp.concatenate([x[:, c * L:(c + 1) * L] for c in range(nc)], axis=0)

    gt = gt_ref[0, 0]
    lf3 = jnp.concatenate(_split3(stack(_log_sigmoid(gt) * LOG2E)), axis=0)
    n_st = nc * GATES_PG
    pre = _dot(lf3, upp01)
    suf = _dot(lf3, low01)
    pre = pre[:n_st] + pre[n_st:2 * n_st] + pre[2 * n_st:]
    suf = suf[:n_st] + suf[n_st:2 * n_st] + suf[2 * n_st:]
    fwd_row = (lax.broadcasted_iota(jnp.int32, (n_st, 1), 0) % GATES_PG) < GATES_PG // 2
    bq = jnp.where(fwd_row, pre, suf)
    i2 = stack(gt * LOG2E)
    ib = []
    for c in range(nc):
        r0 = c * GATES_PG
        brow_s[:, c * L:(c + 1) * L] = bq[r0:r0 + GATES_PG]
        ib.append(i2[r0:r0 + nh] - bq[r0 + nh:r0 + 2 * nh])
        ib.append(i2[r0 + 2 * nh:r0 + 3 * nh] - bq[r0 + 3 * nh:r0 + 4 * nh])
    ib3 = jnp.concatenate(_split3(jnp.concatenate(ib, axis=0)), axis=0)
    ibt = _dot_nt(eye01, ib3)
    n_ib = nc * 2 * nh
    ibcol_s[...] = ibt[:, :n_ib] + ibt[:, n_ib:2 * n_ib] + ibt[:, 2 * n_ib:]

    zpad = jnp.zeros((CONV_PAD, LANES), F32)
    pad_s[0:CONV_PAD, :] = zpad
    pad_s[CONV_PAD + s:2 * CONV_PAD + s, :] = zpad
    for src_ref, cw_ref, cb_ref, dst_s, gain in ((q_ref, cwq_ref, cbq_ref, qs_s, 1.0),
                                                 (k_ref, cwk_ref, cbk_ref, ks_s, MLSTM_QK_DIM ** -0.5)):
        for cb in range(nh * MLSTM_QK_DIM // LANES):
            cols = slice(cb * LANES, (cb + 1) * LANES)
            pad_s[CONV_PAD:CONV_PAD + s, :] = src_ref[0, :, cols].astype(F32)
            cw = cw_ref[:, cols]
            cbias = cb_ref[:, cols]
            for t in range(s // CONV_ROWS):
                r0 = t * CONV_ROWS
                acc = cbias + cw[0:1] * pad_s[r0 + CONV_PAD - 2:r0 + CONV_PAD - 2 + CONV_ROWS, :]
                for j in range(1, CONV_K):
                    off = r0 + CONV_PAD - CONV_K // 2 + j
                    acc = acc + cw[j:j + 1] * pad_s[off:off + CONV_ROWS, :]
                vb = (_silu(acc) * gain).astype(BF16) if gain != 1.0 else _silu(acc).astype(BF16)
                dst_s[2 * cb, r0:r0 + CONV_ROWS, :] = vb[:, :MLSTM_QK_DIM]
                dst_s[2 * cb + 1, r0:r0 + CONV_ROWS, :] = vb[:, MLSTM_QK_DIM:]

    ones_rows = jnp.ones((MLSTM_VA - MLSTM_V_DIM, CONV_ROWS), BF16)

    def vt_step(t, carry):
        r0 = pl.multiple_of(t * CONV_ROWS, CONV_ROWS)
        rows = pl.ds(r0, CONV_ROWS)
        for h in range(nh):
            vv = v_ref[0, rows, h * MLSTM_V_DIM:(h + 1) * MLSTM_V_DIM]
            vt_s[h, 0:MLSTM_V_DIM, rows] = vv.astype(F32).T.astype(BF16)
            vt_s[h, MLSTM_V_DIM:MLSTM_VA, rows] = ones_rows
        return carry

    lax.fori_loop(0, s // CONV_ROWS, vt_step, 0)

    def chunk_scores(h, d, ci):
        c = ci if d == 0 else nc - 1 - ci
        rows = slice(c * L, (c + 1) * L)
        sidx = d * nh + h
        state = st_s[sidx]
        q_c = qs_s[h, rows, :]
        k_c = ks_s[h, rows, :]
        kq = _dot_nt(jnp.concatenate([k_c, state.astype(BF16)], axis=0), q_c)
        return rows, state, k_c, kq

    def chunk_finish(h, d, ci, rows, state, k_c, kq):
        c = ci if d == 0 else nc - 1 - ci
        sidx = d * nh + h
        jf = d * 2 * nh + nh + h
        ji = d * 2 * nh + h
        m = m_s[sidx]
        vt_c = vt_s[h, :, rows]
        ib_col = ibcol_s[:, c * 2 * nh + sidx:c * 2 * nh + sidx + 1]
        b_row = brow_s[jf:jf + 1, rows]
        i_row = gt_ref[0, 0, ji:ji + 1, rows] * LOG2E
        b_tot = b_row[:, L - 1:L] if d == 0 else b_row[:, 0:1]
        a_row = b_row + m
        dt = jnp.where(upp_b if d == 0 else low_b, b_row + ib_col, -jnp.inf)
        m_t = jnp.maximum(a_row, jnp.max(dt, axis=0, keepdims=True))
        w_inter = jnp.exp2(a_row - m_t)
        st = kq[:L] * jnp.exp2(dt - m_t)
        r = _dot(vt_c, st.astype(BF16)) + kq[L:] * w_inter
        den = r[MLSTM_V_DIM:MLSTM_V_DIM + 1]
        scale = 1.0 / jnp.maximum(jnp.abs(den), jnp.exp2(-m_t))
        ht_s[h, :, rows] += r[:MLSTM_V_DIM] * scale
        g_row = b_tot - b_row + i_row
        m_new = jnp.maximum(b_tot + m, jnp.max(g_row, axis=-1, keepdims=True))
        decay = jnp.exp2(b_tot + m - m_new)
        wk = jnp.exp2(g_row - m_new)
        vw = vt_c * wk.astype(BF16)
        st_s[sidx] = decay * state + _dot(vw, k_c)
        m_s[sidx] = m_new

    chains = [(h, d) for h in range(nh) for d in range(2)]
    for h, d in chains:
        st_s[d * nh + h] = jnp.zeros((MLSTM_VA, MLSTM_QK_DIM), F32)
        m_s[d * nh + h] = jnp.zeros((1, 1), F32)
    for h in range(nh):
        ht_s[h] = jnp.zeros((MLSTM_V_DIM, s), F32)

    for ci in range(nc):
        staged = [chunk_scores(h, d, ci) for h, d in chains]
        for (h, d), args in zip(chains, staged):
            chunk_finish(h, d, ci, *args)

    def fin(t, carry):
        r0 = pl.multiple_of(t * CONV_ROWS, CONV_ROWS)
        rows = pl.ds(r0, CONV_ROWS)
        hsts = [ht_s[h, :, rows] for h in range(nh)]
        hns = [(x * lax.rsqrt(jnp.mean(x * x, axis=0, keepdims=True) + EPS)).T for x in hsts]
        for h in range(nh):
            cols = slice(h * MLSTM_V_DIM, (h + 1) * MLSTM_V_DIM)
            og = _sigmoid(o_ref[0, rows, cols].astype(F32))
            y_ref[0, rows, cols] = (og * (hns[h] * gain_ref[:, cols])).astype(BF16)
        return carry

    lax.fori_loop(0, s // CONV_ROWS, fin, 0)


def _mlstm_call(q, k, v, o, gt, conv_w, conv_b, head_gain):
    b, s, _ = q.shape
    n_hg = MLSTM_HEADS // MLSTM_HPG
    qw = MLSTM_HPG * MLSTM_QK_DIM
    vw = MLSTM_HPG * MLSTM_V_DIM
    gt = gt.reshape(b, n_hg, GATES_PG, s)
    return pl.pallas_call(
        _mlstm_kernel,
        out_shape=jax.ShapeDtypeStruct((b, s, MLSTM_V_WIDTH), BF16),
        grid=(b, n_hg),
        in_specs=[
            pl.BlockSpec((1, s, qw), lambda i, g: (i, 0, g)),
            pl.BlockSpec((1, s, qw), lambda i, g: (i, 0, g)),
            pl.BlockSpec((1, s, vw), lambda i, g: (i, 0, g)),
            pl.BlockSpec((1, s, vw), lambda i, g: (i, 0, g)),
            pl.BlockSpec((1, 1, GATES_PG, s), lambda i, g: (i, g, 0, 0)),
            pl.BlockSpec((CONV_K, qw), lambda i, g: (0, g)),
            pl.BlockSpec((CONV_K, qw), lambda i, g: (0, n_hg + g)),
            pl.BlockSpec((1, qw), lambda i, g: (0, g)),
            pl.BlockSpec((1, qw), lambda i, g: (0, n_hg + g)),
            pl.BlockSpec((1, vw), lambda i, g: (0, g)),
        ],
        out_specs=pl.BlockSpec((1, s, vw), lambda i, g: (i, 0, g)),
        scratch_shapes=[
            pltpu.VMEM((s + 2 * CONV_PAD, LANES), F32),
            pltpu.VMEM((MLSTM_HPG, s, MLSTM_QK_DIM), BF16),
            pltpu.VMEM((MLSTM_HPG, s, MLSTM_QK_DIM), BF16),
            pltpu.VMEM((MLSTM_HPG, MLSTM_VA, s), BF16),
            pltpu.VMEM((MLSTM_CHUNK, (s // MLSTM_CHUNK) * 2 * MLSTM_HPG), F32),
            pltpu.VMEM((GATES_PG, s), F32),
            pltpu.VMEM((2 * MLSTM_HPG, MLSTM_VA, MLSTM_QK_DIM), F32),
            pltpu.VMEM((2 * MLSTM_HPG, 1, 1), F32),
            pltpu.VMEM((MLSTM_HPG, MLSTM_V_DIM, s), F32),
        ],
        compiler_params=_cparams("parallel", "parallel"),
        name="mlstm_mixer",
    )(q, k, v, o, gt, conv_w, conv_w, conv_b, conv_b, head_gain)


FFN_SLABS = ((0, 1280), (1280, 2816))


def _post_kernel(x_ref, y_ref, mod_ref, wo_ref, w1_ref, w3_ref, w2_ref, o_ref):
    m = mod_ref[0]
    g1, sh2, sc2, g2 = m[2:3], m[3:4], m[4:5], m[5:6]
    x1 = x_ref[0] + g1 * _dot(y_ref[0], wo_ref[...])
    h = _norm_mod(x1, sh2, sc2).astype(BF16)
    acc = None
    for lo, hi in FFN_SLABS:
        a = _dot(h, w1_ref[:, lo:hi])
        b = _dot(h, w3_ref[:, lo:hi])
        gact = (_silu(a) * b).astype(BF16)
        contrib = _dot(gact, w2_ref[lo:hi, :])
        acc = contrib if acc is None else acc + contrib
    o_ref[0] = x1 + g2 * acc


def _post_call(x, y, mod, w_out, w1, w3, w2):
    b, s, d = x.shape
    tm = min(512, s)
    const2 = lambda i, j: (0, 0)
    assert FFN_SLABS[-1][1] == w1.shape[1]
    return pl.pallas_call(
        _post_kernel,
        out_shape=jax.ShapeDtypeStruct((b, s, d), F32),
        grid=(b, s // tm),
        in_specs=[
            pl.BlockSpec((1, tm, d), lambda i, j: (i, j, 0)),
            pl.BlockSpec((1, tm, y.shape[2]), lambda i, j: (i, j, 0)),
            pl.BlockSpec((1, 6, d), lambda i, j: (i, 0, 0)),
            pl.BlockSpec(w_out.shape, lambda i, j: (0, 0)),
            pl.BlockSpec(w1.shape, const2, pipeline_mode=pl.Buffered(1)),
            pl.BlockSpec(w3.shape, const2, pipeline_mode=pl.Buffered(1)),
            pl.BlockSpec(w2.shape, const2, pipeline_mode=pl.Buffered(1)),
        ],
        out_specs=pl.BlockSpec((1, tm, d), lambda i, j: (i, j, 0)),
        compiler_params=_cparams("parallel", "parallel"),
        name="post_ffn",
    )(x, y, mod, w_out, w1, w3, w2)


def _rope_tables(s):
    rows = s // GRID_W
    row = jnp.repeat(jnp.arange(rows), GRID_W).astype(F32)
    col = jnp.tile(jnp.arange(GRID_W), rows).astype(F32)
    n_freq = ATT_HEAD_DIM // 4
    inv_freq = ROPE_THETA ** (-jnp.arange(n_freq, dtype=F32) / n_freq)
    ang = jnp.concatenate([row[:, None] * inv_freq, col[:, None] * inv_freq], axis=-1)
    cos, sin = jnp.cos(ang), jnp.sin(ang)
    cos_t = jnp.tile(cos, (1, LANES // cos.shape[1]))
    sin_t = jnp.tile(jnp.concatenate([-sin, sin], axis=-1), (1, LANES // ATT_HEAD_DIM))
    return cos_t, sin_t


def _pool_band():
    r = np.arange(POOL_TILE)[:, None]
    c = np.arange(POOL_TILE + 2 * POOL_HALO)[None, :]
    bands = []
    for w in POOL_WINDOWS:
        off = c - POOL_HALO - r + w // 2
        bands.append((off >= 0) & (off < w))
    return jnp.asarray(np.stack(bands), dtype=BF16)


def _head_split_perm():
    return np.concatenate([np.arange(0, ATT_HEAD_DIM, 2), np.arange(1, ATT_HEAD_DIM, 2)])


def kernel(x, c, ada_w, ada_b, ev_w_in, ev_w_pool, ev_b_pool, ev_pool_scale, ev_q_gain, ev_k_gain,
           ev_w_out, od_w_in, od_conv_w, od_conv_b, od_gate_b, od_head_gain, od_w_out,
           ffn_w1, ffn_w3, ffn_w2):
    b, s, d = x.shape
    depth = ada_w.shape[0]
    mod = _ada_call(c, ada_w, ada_b).reshape(depth, b, 6, d)
    cos_t, sin_t = _rope_tables(s)
    band = _pool_band()
    perm = _head_split_perm()
    qk_cols = np.concatenate([POOL_DIM + hh * ATT_HEAD_DIM + perm for hh in range(ATT_HEADS + ATT_KV_HEADS)])
    col_perm = np.concatenate([np.arange(POOL_DIM), qk_cols, np.arange(POOL_DIM + ATT_Q_DIM + ATT_KV_DIM,
                                                                      POOL_DIM + ATT_Q_DIM + 2 * ATT_KV_DIM)])
    gate_perm = np.arange(N_GATES).reshape(2, 2, MLSTM_HEADS // MLSTM_HPG, MLSTM_HPG).transpose(2, 0, 1, 3).reshape(-1)

    for layer in range(depth):
        j = layer // 2
        if layer % 2 == 0:
            w_in = ev_w_in[j][:, col_perm].astype(BF16)
            u, q, kv = _inproj_even_call(x, mod[layer], w_in)
            qg = jnp.tile(ev_q_gain[j][perm], LANES // ATT_HEAD_DIM).reshape(1, LANES)
            kg = jnp.tile(ev_k_gain[j][perm], LANES // ATT_HEAD_DIM).reshape(1, LANES)
            y = _even_mixer_call(u, q, kv, cos_t, sin_t, qg, kg, band, ev_w_pool[j].astype(BF16),
                                 ev_b_pool[j].reshape(-1, 1, POOL_GROUP_DIM),
                                 ev_pool_scale[j].reshape(-1, 1, POOL_GROUP_DIM))
            w_out = ev_w_out[j].astype(BF16)
        else:
            n_main = 2 * MLSTM_QK_WIDTH + 2 * MLSTM_V_WIDTH
            w_in = od_w_in[j][:, :n_main].astype(BF16)
            wgt = od_w_in[j][:, n_main:][:, gate_perm].T.astype(BF16)
            q, k, v, o, gt = _inproj_odd_call(x, mod[layer], w_in, wgt, od_gate_b[j][gate_perm])
            y = _mlstm_call(q, k, v, o, gt, od_conv_w[j], od_conv_b[j].reshape(1, -1),
                            od_head_gain[j].reshape(1, -1))
            w_out = od_w_out[j].astype(BF16)
        x = _post_call(x, y, mod[layer], w_out, ffn_w1[layer].astype(BF16), ffn_w3[layer].astype(BF16),
                       ffn_w2[layer].astype(BF16))
    return x
```

```python
import jax
import jax.numpy as jnp
import numpy as np
from jax import lax
from jax.experimental import pallas as pl
from jax.experimental.pallas import tpu as pltpu

F32 = jnp.float32
BF16 = jnp.bfloat16

EPS = 1e-6
GRID_W = 64
ROPE_THETA = 10000.0
POOL_WINDOWS = (2, 4, 8, 16)
POOL_GROUP_DIM = 128
POOL_DIM = 512
ATT_HEADS = 8
ATT_KV_HEADS = 2
ATT_GROUP = ATT_HEADS // ATT_KV_HEADS
ATT_HEAD_DIM = 64
ATT_Q_DIM = 512
ATT_KV_DIM = 128
MLSTM_HEADS = 8
MLSTM_QK_DIM = 64
MLSTM_V_DIM = 128
MLSTM_QK_WIDTH = 512
MLSTM_V_WIDTH = 1024
N_GATES = 32
CONV_K = 5

LANES = 128
POOL_HALO = 16
POOL_TILE = 256
CONV_PAD = 8
VMEM_LIMIT = 56 * 1024 * 1024


def _cparams(*sem):
    return pltpu.CompilerParams(dimension_semantics=sem, vmem_limit_bytes=VMEM_LIMIT)


def _sigmoid(x):
    return 1.0 / (1.0 + jnp.exp(-x))


def _silu(x):
    return x * _sigmoid(x)


def _log_sigmoid(x):
    return jnp.minimum(x, 0.0) - jnp.log(1.0 + jnp.exp(-jnp.abs(x)))


def _norm_mod(x, shift, scale):
    ms = jnp.mean(x * x, axis=-1, keepdims=True)
    return (x * lax.rsqrt(ms + EPS)) * (1.0 + scale) + shift


def _split3(x):
    x1 = x.astype(BF16)
    r1 = x - x1.astype(F32)
    x2 = r1.astype(BF16)
    x3 = (r1 - x2.astype(F32)).astype(BF16)
    return x1, x2, x3


def _dot(a, b):
    return jnp.dot(a, b, preferred_element_type=F32)


def _dot_nt(a, b):
    return lax.dot_general(a, b, (((1,), (1,)), ((), ())), preferred_element_type=F32)


def _ada_kernel(c_ref, w_ref, b_ref, o_ref):
    cond = _silu(c_ref[...]).astype(BF16)
    o_ref[0] = _dot(cond, w_ref[0].astype(BF16)) + b_ref[0]


def _ada_call(c, ada_w, ada_b):
    depth, d, n = ada_w.shape
    b = c.shape[0]
    tn = 1024
    return pl.pallas_call(
        _ada_kernel,
        out_shape=jax.ShapeDtypeStruct((depth, b, n), F32),
        grid=(depth, n // tn),
        in_specs=[
            pl.BlockSpec((b, d), lambda l, j: (0, 0)),
            pl.BlockSpec((1, d, tn), lambda l, j: (l, 0, j)),
            pl.BlockSpec((1, 1, tn), lambda l, j: (l, 0, j)),
        ],
        out_specs=pl.BlockSpec((1, b, tn), lambda l, j: (l, 0, j)),
        compiler_params=_cparams("arbitrary", "arbitrary"),
        name="ada_table",
    )(c, ada_w, ada_b.reshape(depth, 1, n))


def _inproj_even_kernel(x_ref, mod_ref, w_ref, u_ref, q_ref, kv_ref):
    m = mod_ref[0]
    h = _norm_mod(x_ref[0], m[0:1], m[1:2]).astype(BF16)
    z = _dot(h, w_ref[...]).astype(BF16)
    u_ref[0] = z[:, :POOL_DIM]
    q_ref[0] = z[:, POOL_DIM:POOL_DIM + ATT_Q_DIM]
    kv_ref[0] = z[:, POOL_DIM + ATT_Q_DIM:]


def _inproj_even_call(x, mod, w):
    b, s, d = x.shape
    n = w.shape[1]
    tm = min(2048, s)
    widths = (POOL_DIM, ATT_Q_DIM, 2 * ATT_KV_DIM)
    return pl.pallas_call(
        _inproj_even_kernel,
        out_shape=tuple(jax.ShapeDtypeStruct((b, s, wd), BF16) for wd in widths),
        grid=(b, s // tm),
        in_specs=[
            pl.BlockSpec((1, tm, d), lambda i, j: (i, j, 0)),
            pl.BlockSpec((1, 6, d), lambda i, j: (i, 0, 0)),
            pl.BlockSpec((d, n), lambda i, j: (0, 0)),
        ],
        out_specs=tuple(pl.BlockSpec((1, tm, wd), lambda i, j: (i, j, 0)) for wd in widths),
        compiler_params=_cparams("parallel", "parallel"),
        name="inproj_even",
    )(x, mod, w)


def _inproj_odd_kernel(x_ref, mod_ref, w_ref, wgt_ref, gbt_ref, q_ref, k_ref, v_ref, o_ref, gt_ref):
    m = mod_ref[0]
    h = _norm_mod(x_ref[0], m[0:1], m[1:2]).astype(BF16)
    z = _dot(h, w_ref[...]).astype(BF16)
    q_ref[0] = z[:, :MLSTM_QK_WIDTH]
    k_ref[0] = z[:, MLSTM_QK_WIDTH:2 * MLSTM_QK_WIDTH]
    v_ref[0] = z[:, 2 * MLSTM_QK_WIDTH:2 * MLSTM_QK_WIDTH + MLSTM_V_WIDTH]
    o_ref[0] = z[:, 2 * MLSTM_QK_WIDTH + MLSTM_V_WIDTH:]
    gt_ref[0] = _dot_nt(wgt_ref[...], h) + gbt_ref[...]


def _inproj_odd_call(x, mod, w, wgt, gate_b):
    b, s, d = x.shape
    n = w.shape[1]
    tm = min(1024, s)
    widths = (MLSTM_QK_WIDTH, MLSTM_QK_WIDTH, MLSTM_V_WIDTH, MLSTM_V_WIDTH)
    return pl.pallas_call(
        _inproj_odd_kernel,
        out_shape=tuple(jax.ShapeDtypeStruct((b, s, wd), BF16) for wd in widths)
        + (jax.ShapeDtypeStruct((b, N_GATES, s), F32),),
        grid=(b, s // tm),
        in_specs=[
            pl.BlockSpec((1, tm, d), lambda i, j: (i, j, 0)),
            pl.BlockSpec((1, 6, d), lambda i, j: (i, 0, 0)),
            pl.BlockSpec((d, n), lambda i, j: (0, 0)),
            pl.BlockSpec((N_GATES, d), lambda i, j: (0, 0)),
            pl.BlockSpec((N_GATES, 1), lambda i, j: (0, 0)),
        ],
        out_specs=tuple(pl.BlockSpec((1, tm, wd), lambda i, j: (i, j, 0)) for wd in widths)
        + (pl.BlockSpec((1, N_GATES, tm), lambda i, j: (i, 0, j)),),
        compiler_params=_cparams("parallel", "parallel"),
        name="inproj_odd",
    )(x, mod, w, wgt, gate_b.reshape(N_GATES, 1))


ATT_QB = 256
ATT_PREP = 256
LOG2E = 1.4426950408889634
ATT_UNROLL = 8
ATT_KB = 512
ATT_VA = ATT_HEAD_DIM + 16


def _seg_mean_sq(x, bd):
    x2 = x * x
    hi = x2.astype(BF16)
    lo = (x2 - hi.astype(F32)).astype(BF16)
    return (_dot(hi, bd) + _dot(lo, bd)) * (1.0 / ATT_HEAD_DIM)


def _rope_swap(x, first_half):
    return jnp.where(first_half, pltpu.roll(x, LANES - 32, 1), pltpu.roll(x, 32, 1))


def _even_mixer_kernel(u_ref, q_ref, kv_ref, cos_ref, sin_ref, qg_ref, kg_ref, band_ref,
                       wp_ref, bp_ref, ps_ref, y_ref, up_s, qt_s, ks_s, vt_s, st_s, mx_s):
    s = u_ref.shape[1]
    lane = lax.broadcasted_iota(jnp.int32, (1, LANES), 1)
    first_half = (lane % ATT_HEAD_DIM) < (ATT_HEAD_DIM // 2)
    r_i = lax.broadcasted_iota(jnp.int32, (LANES, LANES), 0)
    c_i = lax.broadcasted_iota(jnp.int32, (LANES, LANES), 1)
    bd = jnp.where((r_i // ATT_HEAD_DIM) == (c_i // ATT_HEAD_DIM), 1.0, 0.0).astype(BF16)

    def prep(t, carry):
        r0 = pl.multiple_of(t * ATT_PREP, ATT_PREP)
        rows = pl.ds(r0, ATT_PREP)
        cos = cos_ref[rows, :]
        sin = sin_ref[rows, :]

        n_qc = ATT_Q_DIM // LANES
        xs = [q_ref[0, rows, cb * LANES:(cb + 1) * LANES].astype(F32) for cb in range(n_qc)]
        xs.append(kv_ref[0, rows, 0:LANES].astype(F32))
        gains = [qg_ref[...]] * n_qc + [kg_ref[...]]
        mss = [_seg_mean_sq(xb, bd) for xb in xs]
        roped = []
        for xb, ms, gain in zip(xs, mss, gains):
            xn = xb * lax.rsqrt(ms + EPS) * gain
            roped.append(xn * cos + _rope_swap(xn, first_half) * sin)
        for cb in range(n_qc):
            qt = (roped[cb] * (ATT_HEAD_DIM ** -0.5 * LOG2E)).T.astype(BF16)
            qt_s[2 * cb, :, rows] = qt[:ATT_HEAD_DIM]
            qt_s[2 * cb + 1, :, rows] = qt[ATT_HEAD_DIM:]
        kn = roped[n_qc].astype(BF16)
        ks_s[0, rows, :] = kn[:, :ATT_HEAD_DIM]
        ks_s[1, rows, :] = kn[:, ATT_HEAD_DIM:]
        vt = kv_ref[0, rows, LANES:2 * LANES].astype(F32).T.astype(BF16)
        ones = jnp.ones((ATT_VA - ATT_HEAD_DIM, ATT_PREP), BF16)
        vt_s[0, :, rows] = jnp.concatenate([vt[:ATT_HEAD_DIM], ones], axis=0)
        vt_s[1, :, rows] = jnp.concatenate([vt[ATT_HEAD_DIM:], ones], axis=0)
        return carry

    lax.fori_loop(0, s // ATT_PREP, prep, 0)

    n_qb = s // ATT_QB
    n_steps = ATT_KV_HEADS * (ATT_GROUP // 2) * n_qb

    def decode(t):
        j = t // ((ATT_GROUP // 2) * n_qb)
        gp = (t // n_qb) % (ATT_GROUP // 2)
        rows = pl.ds(pl.multiple_of((t % n_qb) * ATT_QB, ATT_QB), ATT_QB)
        return j, ATT_GROUP * j + 2 * gp, rows

    n_kb = s // ATT_KB

    def att_step(t_next, t_cur, slot_next, slot_cur):
        jn, hn, rows_n = decode(t_next)
        jc, hc, rows_c = decode(t_cur)
        qt2 = jnp.concatenate([qt_s[hn, :, rows_n], qt_s[hn + 1, :, rows_n]], axis=1)
        mx_cur = mx_s[slot_cur]
        mx_next = None
        acc = [None, None]
        for kb in range(n_kb):
            keys = slice(kb * ATT_KB, (kb + 1) * ATT_KB)
            st = _dot(ks_s[jn, keys, :], qt2)
            st_s[slot_next, keys, :] = st
            bmx = jnp.max(st, axis=0, keepdims=True)
            mx_next = bmx if mx_next is None else jnp.maximum(mx_next, bmx)
            p = jnp.exp2(st_s[slot_cur, keys, :] - mx_cur).astype(BF16)
            for g in range(2):
                part = _dot(vt_s[jc, :, keys], p[:, g * ATT_QB:(g + 1) * ATT_QB])
                acc[g] = part if acc[g] is None else acc[g] + part
        mx_s[slot_next] = mx_next
        outs = [a[:ATT_HEAD_DIM] / a[ATT_HEAD_DIM:ATT_HEAD_DIM + 1] for a in acc]
        pair = jnp.concatenate(outs, axis=0).T
        c0 = pl.multiple_of(POOL_DIM + hc * ATT_HEAD_DIM, LANES)
        y_ref[0, rows_c, pl.ds(c0, LANES)] = pair.astype(BF16)

    j0, h00, rows0 = decode(0)
    st0 = _dot(ks_s[j0], jnp.concatenate([qt_s[h00, :, rows0], qt_s[h00 + 1, :, rows0]], axis=1))
    st_s[0] = st0
    mx_s[0] = jnp.max(st0, axis=0, keepdims=True)

    def att_body(k, carry):
        t0 = ATT_UNROLL * k
        for u in range(ATT_UNROLL):
            att_step(jnp.minimum(t0 + u + 1, n_steps - 1), t0 + u, (u + 1) % 2, u % 2)
        return carry

    lax.fori_loop(0, n_steps // ATT_UNROLL, att_body, 0)

    zeros_h = jnp.zeros((POOL_HALO, POOL_DIM), BF16)
    up_s[0:POOL_HALO, :] = zeros_h
    up_s[POOL_HALO + s:2 * POOL_HALO + s, :] = zeros_h
    up_s[POOL_HALO:POOL_HALO + s, :] = u_ref[0]

    def pool_step(t, carry):
        r0 = pl.multiple_of(t * POOL_TILE, POOL_TILE)
        tok = r0 + lax.broadcasted_iota(jnp.int32, (POOL_TILE, 1), 0)
        groups = list(enumerate(POOL_WINDOWS))
        cols = [slice(g * POOL_GROUP_DIM, (g + 1) * POOL_GROUP_DIM) for g, _ in groups]
        tots = [_dot(band_ref[g], up_s[pl.ds(r0, POOL_TILE + 2 * POOL_HALO), cols[g]]) for g, _ in groups]
        pooled = []
        for g, w in groups:
            lo = jnp.maximum(tok - w // 2, 0)
            hi = jnp.minimum(tok + (w - w // 2), s)
            cnt = (hi - lo).astype(F32)
            ug = u_ref[0, pl.ds(r0, POOL_TILE), cols[g]].astype(F32)
            pooled.append((tots[g] / cnt - ug).astype(BF16))
        proj = [_dot(pooled[g], wp_ref[g]) for g, _ in groups]
        for g, _ in groups:
            y_ref[0, pl.ds(r0, POOL_TILE), cols[g]] = ((proj[g] + bp_ref[g]) * ps_ref[g]).astype(BF16)
        return carry

    lax.fori_loop(0, s // POOL_TILE, pool_step, 0)


def _even_mixer_call(u, q, kv, cos_t, sin_t, q_gain, k_gain, band, w_pool, b_pool, pool_scale):
    b, s, _ = u.shape
    return pl.pallas_call(
        _even_mixer_kernel,
        out_shape=jax.ShapeDtypeStruct((b, s, POOL_DIM + ATT_Q_DIM), BF16),
        grid=(b,),
        in_specs=[
            pl.BlockSpec((1, s, POOL_DIM), lambda i: (i, 0, 0)),
            pl.BlockSpec((1, s, ATT_Q_DIM), lambda i: (i, 0, 0)),
            pl.BlockSpec((1, s, 2 * ATT_KV_DIM), lambda i: (i, 0, 0)),
            pl.BlockSpec((s, LANES), lambda i: (0, 0)),
            pl.BlockSpec((s, LANES), lambda i: (0, 0)),
            pl.BlockSpec((1, LANES), lambda i: (0, 0)),
            pl.BlockSpec((1, LANES), lambda i: (0, 0)),
            pl.BlockSpec(band.shape, lambda i: (0, 0, 0)),
            pl.BlockSpec(w_pool.shape, lambda i: (0, 0, 0)),
            pl.BlockSpec(b_pool.shape, lambda i: (0, 0, 0)),
            pl.BlockSpec(pool_scale.shape, lambda i: (0, 0, 0)),
        ],
        out_specs=pl.BlockSpec((1, s, POOL_DIM + ATT_Q_DIM), lambda i: (i, 0, 0)),
        scratch_shapes=[
            pltpu.VMEM((s + 2 * POOL_HALO, POOL_DIM), BF16),
            pltpu.VMEM((ATT_HEADS, ATT_HEAD_DIM, s), BF16),
            pltpu.VMEM((ATT_KV_HEADS, s, ATT_HEAD_DIM), BF16),
            pltpu.VMEM((ATT_KV_HEADS, ATT_VA, s), BF16),
            pltpu.VMEM((2, s, 2 * ATT_QB), F32),
            pltpu.VMEM((2, 1, 2 * ATT_QB), F32),
        ],
        compiler_params=_cparams("parallel"),
        name="even_mixer",
    )(u, q, kv, cos_t, sin_t, q_gain, k_gain, band, w_pool, b_pool, pool_scale)


CONV_ROWS = 256
MLSTM_HPG = 4
MLSTM_VA = MLSTM_V_DIM + 16
MLSTM_CHUNK = 256
GATES_PG = 4 * MLSTM_HPG


def _mlstm_kernel(q_ref, k_ref, v_ref, o_ref, gt_ref, cwq_ref, cwk_ref, cbq_ref, cbk_ref, gain_ref, y_ref,
                  pad_s, qt_s, ks_s, vt_s, ibcol_s, brow_s, st_s, m_s, ht_s):
    s = q_ref.shape[1]
    L = MLSTM_CHUNK
    nc = s // L
    nh = MLSTM_HPG
    r_i = lax.broadcasted_iota(jnp.int32, (L, L), 0)
    c_i = lax.broadcasted_iota(jnp.int32, (L, L), 1)
    low_b = r_i >= c_i
    upp_b = c_i >= r_i
    low01 = jnp.where(low_b, 1.0, 0.0).astype(BF16)
    upp01 = jnp.where(upp_b, 1.0, 0.0).astype(BF16)
    eye01 = jnp.where(r_i == c_i, 1.0, 0.0).astype(BF16)

    def stack(x):
        return jnp.concatenate([x[:, c * L:(c + 1) * L] for c in range(nc)], axis=0)

    gt = gt_ref[0, 0]
    lf3 = jnp.concatenate(_split3(stack(_log_sigmoid(gt) * LOG2E)), axis=0)
    n_st = nc * GATES_PG
    pre = _dot(lf3, upp01)
    suf = _dot(lf3, low01)
    pre = pre[:n_st] + pre[n_st:2 * n_st] + pre[2 * n_st:]
    suf = suf[:n_st] + suf[n_st:2 * n_st] + suf[2 * n_st:]
    fwd_row = (lax.broadcasted_iota(jnp.int32, (n_st, 1), 0) % GATES_PG) < GATES_PG // 2
    bq = jnp.where(fwd_row, pre, suf)
    i2 = stack(gt * LOG2E)
    ib = []
    for c in range(nc):
        r0 = c * GATES_PG
        brow_s[:, c * L:(c + 1) * L] = bq[r0:r0 + GATES_PG]
        ib.append(i2[r0:r0 + nh] - bq[r0 + nh:r0 + 2 * nh])
        ib.append(i2[r0 + 2 * nh:r0 + 3 * nh] - bq[r0 + 3 * nh:r0 + 4 * nh])
    ib3 = jnp.concatenate(_split3(jnp.concatenate(ib, axis=0)), axis=0)
    ibt = _dot_nt(eye01, ib3)
    n_ib = nc * 2 * nh
    ibcol_s[...] = ibt[:, :n_ib] + ibt[:, n_ib:2 * n_ib] + ibt[:, 2 * n_ib:]

    zpad = jnp.zeros((CONV_PAD, LANES), F32)
    pad_s[0:CONV_PAD, :] = zpad
    pad_s[CONV_PAD + s:2 * CONV_PAD + s, :] = zpad
    for src_ref, cw_ref, cb_ref, dst_s, gain, transposed in (
            (q_ref, cwq_ref, cbq_ref, qt_s, 1.0, True),
            (k_ref, cwk_ref, cbk_ref, ks_s, MLSTM_QK_DIM ** -0.5, False)):
        for cb in range(nh * MLSTM_QK_DIM // LANES):
            cols = slice(cb * LANES, (cb + 1) * LANES)
            pad_s[CONV_PAD:CONV_PAD + s, :] = src_ref[0, :, cols].astype(F32)
            cw = cw_ref[:, cols]
            cbias = cb_ref[:, cols]
            for t in range(s // CONV_ROWS):
                r0 = t * CONV_ROWS
                acc = cbias + cw[0:1] * pad_s[r0 + CONV_PAD - 2:r0 + CONV_PAD - 2 + CONV_ROWS, :]
                for j in range(1, CONV_K):
                    off = r0 + CONV_PAD - CONV_K // 2 + j
                    acc = acc + cw[j:j + 1] * pad_s[off:off + CONV_ROWS, :]
                if transposed:
                    vt = _silu(acc).T.astype(BF16)
                    dst_s[2 * cb, :, r0:r0 + CONV_ROWS] = vt[:MLSTM_QK_DIM]
                    dst_s[2 * cb + 1, :, r0:r0 + CONV_ROWS] = vt[MLSTM_QK_DIM:]
                else:
                    vb = (_silu(acc) * gain).astype(BF16)
                    dst_s[2 * cb, r0:r0 + CONV_ROWS, :] = vb[:, :MLSTM_QK_DIM]
                    dst_s[2 * cb + 1, r0:r0 + CONV_ROWS, :] = vb[:, MLSTM_QK_DIM:]

    ones_rows = jnp.ones((MLSTM_VA - MLSTM_V_DIM, CONV_ROWS), BF16)

    def vt_step(t, carry):
        r0 = pl.multiple_of(t * CONV_ROWS, CONV_ROWS)
        rows = pl.ds(r0, CONV_ROWS)
        for h in range(nh):
            vv = v_ref[0, rows, h * MLSTM_V_DIM:(h + 1) * MLSTM_V_DIM]
            vt_s[h, 0:MLSTM_V_DIM, rows] = vv.astype(F32).T.astype(BF16)
            vt_s[h, MLSTM_V_DIM:MLSTM_VA, rows] = ones_rows
        return carry

    lax.fori_loop(0, s // CONV_ROWS, vt_step, 0)

    def chunk_scores(h, d, ci):
        c = ci if d == 0 else nc - 1 - ci
        rows = slice(c * L, (c + 1) * L)
        sidx = d * nh + h
        state = st_s[sidx]
        qt_c = qt_s[h, :, rows]
        k_c = ks_s[h, rows, :]
        kq = _dot(jnp.concatenate([k_c, state.astype(BF16)], axis=0), qt_c)
        return rows, state, k_c, kq

    def chunk_finish(h, d, ci, rows, state, k_c, kq):
        c = ci if d == 0 else nc - 1 - ci
        sidx = d * nh + h
        jf = d * 2 * nh + nh + h
        ji = d * 2 * nh + h
        m = m_s[sidx]
        vt_c = vt_s[h, :, rows]
        ib_col = ibcol_s[:, c * 2 * nh + sidx:c * 2 * nh + sidx + 1]
        b_row = brow_s[jf:jf + 1, rows]
        i_row = gt_ref[0, 0, ji:ji + 1, rows] * LOG2E
        b_tot = b_row[:, L - 1:L] if d == 0 else b_row[:, 0:1]
        a_row = b_row + m
        dt = jnp.where(upp_b if d == 0 else low_b, b_row + ib_col, -jnp.inf)
        m_t = jnp.maximum(a_row, jnp.max(dt, axis=0, keepdims=True))
        w_inter = jnp.exp2(a_row - m_t)
        st = kq[:L] * jnp.exp2(dt - m_t)
        r = _dot(vt_c, st.astype(BF16)) + kq[L:] * w_inter
        den = r[MLSTM_V_DIM:MLSTM_V_DIM + 1]
        scale = 1.0 / jnp.maximum(jnp.abs(den), jnp.exp2(-m_t))
        ht_s[h, :, rows] += r[:MLSTM_V_DIM] * scale
        g_row = b_tot - b_row + i_row
        m_new = jnp.maximum(b_tot + m, jnp.max(g_row, axis=-1, keepdims=True))
        decay = jnp.exp2(b_tot + m - m_new)
        wk = jnp.exp2(g_row - m_new)
        vw = vt_c * wk.astype(BF16)
        st_s[sidx] = decay * state + _dot(vw, k_c)
        m_s[sidx] = m_new

    chains = [(h, d) for h in range(nh) for d in range(2)]
    for h, d in chains:
        st_s[d * nh + h] = jnp.zeros((MLSTM_VA, MLSTM_QK_DIM), F32)
        m_s[d * nh + h] = jnp.zeros((1, 1), F32)
    for h in range(nh):
        ht_s[h] = jnp.zeros((MLSTM_V_DIM, s), F32)

    for ci in range(nc):
        staged = [chunk_scores(h, d, ci) for h, d in chains]
        for (h, d), args in zip(chains, staged):
            chunk_finish(h, d, ci, *args)

    def fin(t, carry):
        r0 = pl.multiple_of(t * CONV_ROWS, CONV_ROWS)
        rows = pl.ds(r0, CONV_ROWS)
        hsts = [ht_s[h, :, rows] for h in range(nh)]
        hns = [(x * lax.rsqrt(jnp.mean(x * x, axis=0, keepdims=True) + EPS)).T for x in hsts]
        for h in range(nh):
            cols = slice(h * MLSTM_V_DIM, (h + 1) * MLSTM_V_DIM)
            og = _sigmoid(o_ref[0, rows, cols].astype(F32))
            y_ref[0, rows, cols] = (og * (hns[h] * gain_ref[:, cols])).astype(BF16)
        return carry

    lax.fori_loop(0, s // CONV_ROWS, fin, 0)


def _mlstm_call(q, k, v, o, gt, conv_w, conv_b, head_gain):
    b, s, _ = q.shape
    n_hg = MLSTM_HEADS // MLSTM_HPG
    qw = MLSTM_HPG * MLSTM_QK_DIM
    vw = MLSTM_HPG * MLSTM_V_DIM
    gt = gt.reshape(b, n_hg, GATES_PG, s)
    return pl.pallas_call(
        _mlstm_kernel,
        out_shape=jax.ShapeDtypeStruct((b, s, MLSTM_V_WIDTH), BF16),
        grid=(b, n_hg),
        in_specs=[
            pl.BlockSpec((1, s, qw), lambda i, g: (i, 0, g)),
            pl.BlockSpec((1, s, qw), lambda i, g: (i, 0, g)),
            pl.BlockSpec((1, s, vw), lambda i, g: (i, 0, g)),
            pl.BlockSpec((1, s, vw), lambda i, g: (i, 0, g)),
            pl.BlockSpec((1, 1, GATES_PG, s), lambda i, g: (i, g, 0, 0)),
            pl.BlockSpec((CONV_K, qw), lambda i, g: (0, g)),
            pl.BlockSpec((CONV_K, qw), lambda i, g: (0, n_hg + g)),
            pl.BlockSpec((1, qw), lambda i, g: (0, g)),
            pl.BlockSpec((1, qw), lambda i, g: (0, n_hg + g)),
            pl.BlockSpec((1, vw), lambda i, g: (0, g)),
        ],
        out_specs=pl.BlockSpec((1, s, vw), lambda i, g: (i, 0, g)),
        scratch_shapes=[
            pltpu.VMEM((s + 2 * CONV_PAD, LANES), F32),
            pltpu.VMEM((MLSTM_HPG, MLSTM_QK_DIM, s), BF16),
            pltpu.VMEM((MLSTM_HPG, s, MLSTM_QK_DIM), BF16),
            pltpu.VMEM((MLSTM_HPG, MLSTM_VA, s), BF16),
            pltpu.VMEM((MLSTM_CHUNK, (s // MLSTM_CHUNK) * 2 * MLSTM_HPG), F32),
            pltpu.VMEM((GATES_PG, s), F32),
            pltpu.VMEM((2 * MLSTM_HPG, MLSTM_VA, MLSTM_QK_DIM), F32),
            pltpu.VMEM((2 * MLSTM_HPG, 1, 1), F32),
            pltpu.VMEM((MLSTM_HPG, MLSTM_V_DIM, s), F32),
        ],
        compiler_params=_cparams("parallel", "parallel"),
        name="mlstm_mixer",
    )(q, k, v, o, gt, conv_w, conv_w, conv_b, conv_b, head_gain)


FFN_SLABS = ((0, 1280), (1280, 2816))


def _post_kernel(x_ref, y_ref, mod_ref, wo_ref, w1_ref, w3_ref, w2_ref, o_ref):
    m = mod_ref[0]
    g1, sh2, sc2, g2 = m[2:3], m[3:4], m[4:5], m[5:6]
    x1 = x_ref[0] + g1 * _dot(y_ref[0], wo_ref[...])
    h = _norm_mod(x1, sh2, sc2).astype(BF16)
    acc = None
    for lo, hi in FFN_SLABS:
        a = _dot(h, w1_ref[:, lo:hi])
        b = _dot(h, w3_ref[:, lo:hi])
        gact = (_silu(a) * b).astype(BF16)
        contrib = _dot(gact, w2_ref[lo:hi, :])
        acc = contrib if acc is None else acc + contrib
    o_ref[0] = x1 + g2 * acc


def _post_call(x, y, mod, w_out, w1, w3, w2):
    b, s, d = x.shape
    tm = min(512, s)
    const2 = lambda i, j: (0, 0)
    assert FFN_SLABS[-1][1] == w1.shape[1]
    return pl.pallas_call(
        _post_kernel,
        out_shape=jax.ShapeDtypeStruct((b, s, d), F32),
        grid=(b, s // tm),
        in_specs=[
            pl.BlockSpec((1, tm, d), lambda i, j: (i, j, 0)),
            pl.BlockSpec((1, tm, y.shape[2]), lambda i, j: (i, j, 0)),
            pl.BlockSpec((1, 6, d), lambda i, j: (i, 0, 0)),
            pl.BlockSpec(w_out.shape, lambda i, j: (0, 0)),
            pl.BlockSpec(w1.shape, const2, pipeline_mode=pl.Buffered(1)),
            pl.BlockSpec(w3.shape, const2, pipeline_mode=pl.Buffered(1)),
            pl.BlockSpec(w2.shape, const2, pipeline_mode=pl.Buffered(1)),
        ],
        out_specs=pl.BlockSpec((1, tm, d), lambda i, j: (i, j, 0)),
        compiler_params=_cparams("parallel", "parallel"),
        name="post_ffn",
    )(x, y, mod, w_out, w1, w3, w2)


def _rope_tables(s):
    rows = s // GRID_W
    row = jnp.repeat(jnp.arange(rows), GRID_W).astype(F32)
    col = jnp.tile(jnp.arange(GRID_W), rows).astype(F32)
    n_freq = ATT_HEAD_DIM // 4
    inv_freq = ROPE_THETA ** (-jnp.arange(n_freq, dtype=F32) / n_freq)
    ang = jnp.concatenate([row[:, None] * inv_freq, col[:, None] * inv_freq], axis=-1)
    cos, sin = jnp.cos(ang), jnp.sin(ang)
    cos_t = jnp.tile(cos, (1, LANES // cos.shape[1]))
    sin_t = jnp.tile(jnp.concatenate([-sin, sin], axis=-1), (1, LANES // ATT_HEAD_DIM))
    return cos_t, sin_t


def _pool_band():
    r = np.arange(POOL_TILE)[:, None]
    c = np.arange(POOL_TILE + 2 * POOL_HALO)[None, :]
    bands = []
    for w in POOL_WINDOWS:
        off = c - POOL_HALO - r + w // 2
        bands.append((off >= 0) & (off < w))
    return jnp.asarray(np.stack(bands), dtype=BF16)


def _head_split_perm():
    return np.concatenate([np.arange(0, ATT_HEAD_DIM, 2), np.arange(1, ATT_HEAD_DIM, 2)])


def kernel(x, c, ada_w, ada_b, ev_w_in, ev_w_pool, ev_b_pool, ev_pool_scale, ev_q_gain, ev_k_gain,
           ev_w_out, od_w_in, od_conv_w, od_conv_b, od_gate_b, od_head_gain, od_w_out,
           ffn_w1, ffn_w3, ffn_w2):
    b, s, d = x.shape
    depth = ada_w.shape[0]
    mod = _ada_call(c, ada_w, ada_b).reshape(depth, b, 6, d)
    cos_t, sin_t = _rope_tables(s)
    band = _pool_band()
    perm = _head_split_perm()
    qk_cols = np.concatenate([POOL_DIM + hh * ATT_HEAD_DIM + perm for hh in range(ATT_HEADS + ATT_KV_HEADS)])
    col_perm = np.concatenate([np.arange(POOL_DIM), qk_cols, np.arange(POOL_DIM + ATT_Q_DIM + ATT_KV_DIM,
                                                                      POOL_DIM + ATT_Q_DIM + 2 * ATT_KV_DIM)])
    gate_perm = np.arange(N_GATES).reshape(2, 2, MLSTM_HEADS // MLSTM_HPG, MLSTM_HPG).transpose(2, 0, 1, 3).reshape(-1)

    for layer in range(depth):
        j = layer // 2
        if layer % 2 == 0:
            w_in = ev_w_in[j][:, col_perm].astype(BF16)
            u, q, kv = _inproj_even_call(x, mod[layer], w_in)
            qg = jnp.tile(ev_q_gain[j][perm], LANES // ATT_HEAD_DIM).reshape(1, LANES)
            kg = jnp.tile(ev_k_gain[j][perm], LANES // ATT_HEAD_DIM).reshape(1, LANES)
            y = _even_mixer_call(u, q, kv, cos_t, sin_t, qg, kg, band, ev_w_pool[j].astype(BF16),
                                 ev_b_pool[j].reshape(-1, 1, POOL_GROUP_DIM),
                                 ev_pool_scale[j].reshape(-1, 1, POOL_GROUP_DIM))
            w_out = ev_w_out[j].astype(BF16)
        else:
            n_main = 2 * MLSTM_QK_WIDTH + 2 * MLSTM_V_WIDTH
            w_in = od_w_in[j][:, :n_main].astype(BF16)
            wgt = od_w_in[j][:, n_main:][:, gate_perm].T.astype(BF16)
            q, k, v, o, gt = _inproj_odd_call(x, mod[layer], w_in, wgt, od_gate_b[j][gate_perm])
            y = _mlstm_call(q, k, v, o, gt, od_conv_w[j], od_conv_b[j].reshape(1, -1),
                            od_head_gain[j].reshape(1, -1))
            w_out = od_w_out[j].astype(BF16)
        x = _post_call(x, y, mod[layer], w_out, ffn_w1[layer].astype(BF16), ffn_w3[layer].astype(BF16),
                       ffn_w2[layer].astype(BF16))
    return x
```

```python
import jax
import jax.numpy as jnp
import numpy as np
from jax import lax
from jax.experimental import pallas as pl
from jax.experimental.pallas import tpu as pltpu

F32 = jnp.float32
BF16 = jnp.bfloat16

EPS = 1e-6
GRID_W = 64
ROPE_THETA = 10000.0
POOL_WINDOWS = (2, 4, 8, 16)
POOL_GROUP_DIM = 128
POOL_DIM = 512
ATT_HEADS = 8
ATT_KV_HEADS = 2
ATT_GROUP = ATT_HEADS // ATT_KV_HEADS
ATT_HEAD_DIM = 64
ATT_Q_DIM = 512
ATT_KV_DIM = 128
MLSTM_HEADS = 8
MLSTM_QK_DIM = 64
MLSTM_V_DIM = 128
MLSTM_QK_WIDTH = 512
MLSTM_V_WIDTH = 1024
N_GATES = 32
CONV_K = 5

LANES = 128
POOL_HALO = 16
POOL_TILE = 256
CONV_PAD = 8
VMEM_LIMIT = 56 * 1024 * 1024


def _cparams(*sem):
    return pltpu.CompilerParams(dimension_semantics=sem, vmem_limit_bytes=VMEM_LIMIT)


def _sigmoid(x):
    return 1.0 / (1.0 + jnp.exp(-x))


def _silu(x):
    return x * _sigmoid(x)


def _log_sigmoid(x):
    return jnp.minimum(x, 0.0) - jnp.log(1.0 + jnp.exp(-jnp.abs(x)))


def _norm_mod(x, shift, scale):
    ms = jnp.mean(x * x, axis=-1, keepdims=True)
    return (x * lax.rsqrt(ms + EPS)) * (1.0 + scale) + shift


def _split3(x):
    x1 = x.astype(BF16)
    r1 = x - x1.astype(F32)
    x2 = r1.astype(BF16)
    x3 = (r1 - x2.astype(F32)).astype(BF16)
    return x1, x2, x3


def _dot(a, b):
    return jnp.dot(a, b, preferred_element_type=F32)


def _dot_nt(a, b):
    return lax.dot_general(a, b, (((1,), (1,)), ((), ())), preferred_element_type=F32)


def _ada_kernel(c_ref, w_ref, b_ref, o_ref):
    cond = _silu(c_ref[...]).astype(BF16)
    o_ref[0] = _dot(cond, w_ref[0].astype(BF16)) + b_ref[0]


def _ada_call(c, ada_w, ada_b):
    depth, d, n = ada_w.shape
    b = c.shape[0]
    tn = 1024
    return pl.pallas_call(
        _ada_kernel,
        out_shape=jax.ShapeDtypeStruct((depth, b, n), F32),
        grid=(depth, n // tn),
        in_specs=[
            pl.BlockSpec((b, d), lambda l, j: (0, 0)),
            pl.BlockSpec((1, d, tn), lambda l, j: (l, 0, j)),
            pl.BlockSpec((1, 1, tn), lambda l, j: (l, 0, j)),
        ],
        out_specs=pl.BlockSpec((1, b, tn), lambda l, j: (l, 0, j)),
        compiler_params=_cparams("arbitrary", "arbitrary"),
        name="ada_table",
    )(c, ada_w, ada_b.reshape(depth, 1, n))


INPROJ_ROWS = 512


def _row_groups(tm):
    return [slice(r, r + INPROJ_ROWS) for r in range(0, tm, INPROJ_ROWS)]


def _inproj_even_kernel(x_ref, mod_ref, w_ref, u_ref, q_ref, kv_ref):
    m = mod_ref[0]
    groups = _row_groups(x_ref.shape[1])
    hs = [_norm_mod(x_ref[0, r, :], m[0:1], m[1:2]).astype(BF16) for r in groups]
    for r, h in zip(groups, hs):
        z = _dot(h, w_ref[...]).astype(BF16)
        u_ref[0, r, :] = z[:, :POOL_DIM]
        q_ref[0, r, :] = z[:, POOL_DIM:POOL_DIM + ATT_Q_DIM]
        kv_ref[0, r, :] = z[:, POOL_DIM + ATT_Q_DIM:]


def _inproj_even_call(x, mod, w):
    b, s, d = x.shape
    n = w.shape[1]
    tm = min(2048, s)
    widths = (POOL_DIM, ATT_Q_DIM, 2 * ATT_KV_DIM)
    return pl.pallas_call(
        _inproj_even_kernel,
        out_shape=tuple(jax.ShapeDtypeStruct((b, s, wd), BF16) for wd in widths),
        grid=(b, s // tm),
        in_specs=[
            pl.BlockSpec((1, tm, d), lambda i, j: (i, j, 0)),
            pl.BlockSpec((1, 6, d), lambda i, j: (i, 0, 0)),
            pl.BlockSpec((d, n), lambda i, j: (0, 0)),
        ],
        out_specs=tuple(pl.BlockSpec((1, tm, wd), lambda i, j: (i, j, 0)) for wd in widths),
        compiler_params=_cparams("parallel", "parallel"),
        name="inproj_even",
    )(x, mod, w)


def _inproj_odd_kernel(x_ref, mod_ref, w_ref, wgt_ref, gbt_ref, q_ref, k_ref, v_ref, o_ref, gt_ref):
    m = mod_ref[0]
    groups = _row_groups(x_ref.shape[1])
    hs = [_norm_mod(x_ref[0, r, :], m[0:1], m[1:2]).astype(BF16) for r in groups]
    for r, h in zip(groups, hs):
        z = _dot(h, w_ref[...]).astype(BF16)
        q_ref[0, r, :] = z[:, :MLSTM_QK_WIDTH]
        k_ref[0, r, :] = z[:, MLSTM_QK_WIDTH:2 * MLSTM_QK_WIDTH]
        v_ref[0, r, :] = z[:, 2 * MLSTM_QK_WIDTH:2 * MLSTM_QK_WIDTH + MLSTM_V_WIDTH]
        o_ref[0, r, :] = z[:, 2 * MLSTM_QK_WIDTH + MLSTM_V_WIDTH:]
        gt_ref[0, :, r] = _dot_nt(wgt_ref[...], h) + gbt_ref[...]


def _inproj_odd_call(x, mod, w, wgt, gate_b):
    b, s, d = x.shape
    n = w.shape[1]
    tm = min(1024, s)
    widths = (MLSTM_QK_WIDTH, MLSTM_QK_WIDTH, MLSTM_V_WIDTH, MLSTM_V_WIDTH)
    return pl.pallas_call(
        _inproj_odd_kernel,
        out_shape=tuple(jax.ShapeDtypeStruct((b, s, wd), BF16) for wd in widths)
        + (jax.ShapeDtypeStruct((b, N_GATES, s), F32),),
        grid=(b, s // tm),
        in_specs=[
            pl.BlockSpec((1, tm, d), lambda i, j: (i, j, 0)),
            pl.BlockSpec((1, 6, d), lambda i, j: (i, 0, 0)),
            pl.BlockSpec((d, n), lambda i, j: (0, 0)),
            pl.BlockSpec((N_GATES, d), lambda i, j: (0, 0)),
            pl.BlockSpec((N_GATES, 1), lambda i, j: (0, 0)),
        ],
        out_specs=tuple(pl.BlockSpec((1, tm, wd), lambda i, j: (i, j, 0)) for wd in widths)
        + (pl.BlockSpec((1, N_GATES, tm), lambda i, j: (i, 0, j)),),
        compiler_params=_cparams("parallel", "parallel"),
        name="inproj_odd",
    )(x, mod, w, wgt, gate_b.reshape(N_GATES, 1))


ATT_QB = 256
ATT_PREP = 256
LOG2E = 1.4426950408889634
ATT_UNROLL = 8
ATT_KB = 512
ATT_VA = ATT_HEAD_DIM + 16


def _seg_mean_sq(x, bd):
    x2 = x * x
    hi = x2.astype(BF16)
    lo = (x2 - hi.astype(F32)).astype(BF16)
    return (_dot(hi, bd) + _dot(lo, bd)) * (1.0 / ATT_HEAD_DIM)


def _rope_swap(x, first_half):
    return jnp.where(first_half, pltpu.roll(x, LANES - 32, 1), pltpu.roll(x, 32, 1))


def _even_mixer_kernel(u_ref, q_ref, kv_ref, cos_ref, sin_ref, qg_ref, kg_ref, band_ref,
                       wp_ref, bp_ref, ps_ref, y_ref, up_s, qt_s, ks_s, vt_s, st_s, mx_s):
    s = u_ref.shape[1]
    lane = lax.broadcasted_iota(jnp.int32, (1, LANES), 1)
    first_half = (lane % ATT_HEAD_DIM) < (ATT_HEAD_DIM // 2)
    r_i = lax.broadcasted_iota(jnp.int32, (LANES, LANES), 0)
    c_i = lax.broadcasted_iota(jnp.int32, (LANES, LANES), 1)
    bd = jnp.where((r_i // ATT_HEAD_DIM) == (c_i // ATT_HEAD_DIM), 1.0, 0.0).astype(BF16)

    def prep(t, carry):
        r0 = pl.multiple_of(t * ATT_PREP, ATT_PREP)
        rows = pl.ds(r0, ATT_PREP)
        cos = cos_ref[rows, :]
        sin = sin_ref[rows, :]

        n_qc = ATT_Q_DIM // LANES
        xs = [q_ref[0, rows, cb * LANES:(cb + 1) * LANES].astype(F32) for cb in range(n_qc)]
        xs.append(kv_ref[0, rows, 0:LANES].astype(F32))
        gains = [qg_ref[...]] * n_qc + [kg_ref[...]]
        mss = [_seg_mean_sq(xb, bd) for xb in xs]
        roped = []
        for xb, ms, gain in zip(xs, mss, gains):
            xn = xb * lax.rsqrt(ms + EPS) * gain
            roped.append(xn * cos + _rope_swap(xn, first_half) * sin)
        for cb in range(n_qc):
            qt = (roped[cb] * (ATT_HEAD_DIM ** -0.5 * LOG2E)).T.astype(BF16)
            qt_s[2 * cb, :, rows] = qt[:ATT_HEAD_DIM]
            qt_s[2 * cb + 1, :, rows] = qt[ATT_HEAD_DIM:]
        kn = roped[n_qc].astype(BF16)
        ks_s[0, rows, :] = kn[:, :ATT_HEAD_DIM]
        ks_s[1, rows, :] = kn[:, ATT_HEAD_DIM:]
        vt = kv_ref[0, rows, LANES:2 * LANES].astype(F32).T.astype(BF16)
        ones = jnp.ones((ATT_VA - ATT_HEAD_DIM, ATT_PREP), BF16)
        vt_s[0, :, rows] = jnp.concatenate([vt[:ATT_HEAD_DIM], ones], axis=0)
        vt_s[1, :, rows] = jnp.concatenate([vt[ATT_HEAD_DIM:], ones], axis=0)
        return carry

    lax.fori_loop(0, s // ATT_PREP, prep, 0)

    n_qb = s // ATT_QB
    n_steps = ATT_KV_HEADS * (ATT_GROUP // 2) * n_qb

    def decode(t):
        j = t // ((ATT_GROUP // 2) * n_qb)
        gp = (t // n_qb) % (ATT_GROUP // 2)
        rows = pl.ds(pl.multiple_of((t % n_qb) * ATT_QB, ATT_QB), ATT_QB)
        return j, ATT_GROUP * j + 2 * gp, rows

    n_kb = s // ATT_KB

    def att_step(t_next, t_cur, slot_next, slot_cur):
        jn, hn, rows_n = decode(t_next)
        jc, hc, rows_c = decode(t_cur)
        qt2 = jnp.concatenate([qt_s[hn, :, rows_n], qt_s[hn + 1, :, rows_n]], axis=1)
        mx_cur = mx_s[slot_cur]
        mx_next = None
        acc = [None, None]
        for kb in range(n_kb):
            keys = slice(kb * ATT_KB, (kb + 1) * ATT_KB)
            st = _dot(ks_s[jn, keys, :], qt2)
            st_s[slot_next, keys, :] = st
            bmx = jnp.max(st, axis=0, keepdims=True)
            mx_next = bmx if mx_next is None else jnp.maximum(mx_next, bmx)
            p = jnp.exp2(st_s[slot_cur, keys, :] - mx_cur).astype(BF16)
            for g in range(2):
                part = _dot(vt_s[jc, :, keys], p[:, g * ATT_QB:(g + 1) * ATT_QB])
                acc[g] = part if acc[g] is None else acc[g] + part
        mx_s[slot_next] = mx_next
        outs = [a[:ATT_HEAD_DIM] / a[ATT_HEAD_DIM:ATT_HEAD_DIM + 1] for a in acc]
        pair = jnp.concatenate(outs, axis=0).T
        c0 = pl.multiple_of(POOL_DIM + hc * ATT_HEAD_DIM, LANES)
        y_ref[0, rows_c, pl.ds(c0, LANES)] = pair.astype(BF16)

    j0, h00, rows0 = decode(0)
    st0 = _dot(ks_s[j0], jnp.concatenate([qt_s[h00, :, rows0], qt_s[h00 + 1, :, rows0]], axis=1))
    st_s[0] = st0
    mx_s[0] = jnp.max(st0, axis=0, keepdims=True)

    def att_body(k, carry):
        t0 = ATT_UNROLL * k
        for u in range(ATT_UNROLL):
            att_step(jnp.minimum(t0 + u + 1, n_steps - 1), t0 + u, (u + 1) % 2, u % 2)
        return carry

    lax.fori_loop(0, n_steps // ATT_UNROLL, att_body, 0)

    zeros_h = jnp.zeros((POOL_HALO, POOL_DIM), BF16)
    up_s[0:POOL_HALO, :] = zeros_h
    up_s[POOL_HALO + s:2 * POOL_HALO + s, :] = zeros_h
    up_s[POOL_HALO:POOL_HALO + s, :] = u_ref[0]

    def pool_step(t, carry):
        r0 = pl.multiple_of(t * POOL_TILE, POOL_TILE)
        tok = r0 + lax.broadcasted_iota(jnp.int32, (POOL_TILE, 1), 0)
        groups = list(enumerate(POOL_WINDOWS))
        cols = [slice(g * POOL_GROUP_DIM, (g + 1) * POOL_GROUP_DIM) for g, _ in groups]
        tots = [_dot(band_ref[g], up_s[pl.ds(r0, POOL_TILE + 2 * POOL_HALO), cols[g]]) for g, _ in groups]
        pooled = []
        for g, w in groups:
            lo = jnp.maximum(tok - w // 2, 0)
            hi = jnp.minimum(tok + (w - w // 2), s)
            cnt = (hi - lo).astype(F32)
            ug = u_ref[0, pl.ds(r0, POOL_TILE), cols[g]].astype(F32)
            pooled.append((tots[g] / cnt - ug).astype(BF16))
        proj = [_dot(pooled[g], wp_ref[g]) for g, _ in groups]
        for g, _ in groups:
            y_ref[0, pl.ds(r0, POOL_TILE), cols[g]] = ((proj[g] + bp_ref[g]) * ps_ref[g]).astype(BF16)
        return carry

    lax.fori_loop(0, s // POOL_TILE, pool_step, 0)


def _even_mixer_call(u, q, kv, cos_t, sin_t, q_gain, k_gain, band, w_pool, b_pool, pool_scale):
    b, s, _ = u.shape
    return pl.pallas_call(
        _even_mixer_kernel,
        out_shape=jax.ShapeDtypeStruct((b, s, POOL_DIM + ATT_Q_DIM), BF16),
        grid=(b,),
        in_specs=[
            pl.BlockSpec((1, s, POOL_DIM), lambda i: (i, 0, 0)),
            pl.BlockSpec((1, s, ATT_Q_DIM), lambda i: (i, 0, 0)),
            pl.BlockSpec((1, s, 2 * ATT_KV_DIM), lambda i: (i, 0, 0)),
            pl.BlockSpec((s, LANES), lambda i: (0, 0)),
            pl.BlockSpec((s, LANES), lambda i: (0, 0)),
            pl.BlockSpec((1, LANES), lambda i: (0, 0)),
            pl.BlockSpec((1, LANES), lambda i: (0, 0)),
            pl.BlockSpec(band.shape, lambda i: (0, 0, 0)),
            pl.BlockSpec(w_pool.shape, lambda i: (0, 0, 0)),
            pl.BlockSpec(b_pool.shape, lambda i: (0, 0, 0)),
            pl.BlockSpec(pool_scale.shape, lambda i: (0, 0, 0)),
        ],
        out_specs=pl.BlockSpec((1, s, POOL_DIM + ATT_Q_DIM), lambda i: (i, 0, 0)),
        scratch_shapes=[
            pltpu.VMEM((s + 2 * POOL_HALO, POOL_DIM), BF16),
            pltpu.VMEM((ATT_HEADS, ATT_HEAD_DIM, s), BF16),
            pltpu.VMEM((ATT_KV_HEADS, s, ATT_HEAD_DIM), BF16),
            pltpu.VMEM((ATT_KV_HEADS, ATT_VA, s), BF16),
            pltpu.VMEM((2, s, 2 * ATT_QB), F32),
            pltpu.VMEM((2, 1, 2 * ATT_QB), F32),
        ],
        compiler_params=_cparams("parallel"),
        name="even_mixer",
    )(u, q, kv, cos_t, sin_t, q_gain, k_gain, band, w_pool, b_pool, pool_scale)


CONV_ROWS = 256
MLSTM_HPG = 4
MLSTM_VA = MLSTM_V_DIM + 16
MLSTM_CHUNK = 256
GATES_PG = 4 * MLSTM_HPG


def _mlstm_kernel(q_ref, k_ref, v_ref, o_ref, gt_ref, cwq_ref, cwk_ref, cbq_ref, cbk_ref, gain_ref, y_ref,
                  pad_s, qt_s, ks_s, vt_s, ibcol_s, brow_s, st_s, m_s, ht_s):
    s = q_ref.shape[1]
    L = MLSTM_CHUNK
    nc = s // L
    nh = MLSTM_HPG
    r_i = lax.broadcasted_iota(jnp.int32, (L, L), 0)
    c_i = lax.broadcasted_iota(jnp.int32, (L, L), 1)
    low_b = r_i >= c_i
    upp_b = c_i >= r_i
    low01 = jnp.where(low_b, 1.0, 0.0).astype(BF16)
    upp01 = jnp.where(upp_b, 1.0, 0.0).astype(BF16)
    eye01 = jnp.where(r_i == c_i, 1.0, 0.0).astype(BF16)

    def stack(x):
        return jnp.concatenate([x[:, c * L:(c + 1) * L] for c in range(nc)], axis=0)

    gt = gt_ref[0, 0]
    lf3 = jnp.concatenate(_split3(stack(_log_sigmoid(gt) * LOG2E)), axis=0)
    n_st = nc * GATES_PG
    pre = _dot(lf3, upp01)
    suf = _dot(lf3, low01)
    pre = pre[:n_st] + pre[n_st:2 * n_st] + pre[2 * n_st:]
    suf = suf[:n_st] + suf[n_st:2 * n_st] + suf[2 * n_st:]
    fwd_row = (lax.broadcasted_iota(jnp.int32, (n_st, 1), 0) % GATES_PG) < GATES_PG // 2
    bq = jnp.where(fwd_row, pre, suf)
    i2 = stack(gt * LOG2E)
    ib = []
    for c in range(nc):
        r0 = c * GATES_PG
        brow_s[:, c * L:(c + 1) * L] = bq[r0:r0 + GATES_PG]
        ib.append(i2[r0:r0 + nh] - bq[r0 + nh:r0 + 2 * nh])
        ib.append(i2[r0 + 2 * nh:r0 + 3 * nh] - bq[r0 + 3 * nh:r0 + 4 * nh])
    ib3 = jnp.concatenate(_split3(jnp.concatenate(ib, axis=0)), axis=0)
    ibt = _dot_nt(eye01, ib3)
    n_ib = nc * 2 * nh
    ibcol_s[...] = ibt[:, :n_ib] + ibt[:, n_ib:2 * n_ib] + ibt[:, 2 * n_ib:]

    zpad = jnp.zeros((CONV_PAD, LANES), F32)
    pad_s[0:CONV_PAD, :] = zpad
    pad_s[CONV_PAD + s:2 * CONV_PAD + s, :] = zpad
    for src_ref, cw_ref, cb_ref, dst_s, gain, transposed in (
            (q_ref, cwq_ref, cbq_ref, qt_s, 1.0, True),
            (k_ref, cwk_ref, cbk_ref, ks_s, MLSTM_QK_DIM ** -0.5, False)):
        for cb in range(nh * MLSTM_QK_DIM // LANES):
            cols = slice(cb * LANES, (cb + 1) * LANES)
            pad_s[CONV_PAD:CONV_PAD + s, :] = src_ref[0, :, cols].astype(F32)
            cw = cw_ref[:, cols]
            cbias = cb_ref[:, cols]
            for t in range(s // CONV_ROWS):
                r0 = t * CONV_ROWS
                acc = cbias + cw[0:1] * pad_s[r0 + CONV_PAD - 2:r0 + CONV_PAD - 2 + CONV_ROWS, :]
                for j in range(1, CONV_K):
                    off = r0 + CONV_PAD - CONV_K // 2 + j
                    acc = acc + cw[j:j + 1] * pad_s[off:off + CONV_ROWS, :]
                if transposed:
                    vt = _silu(acc).T.astype(BF16)
                    dst_s[2 * cb, :, r0:r0 + CONV_ROWS] = vt[:MLSTM_QK_DIM]
                    dst_s[2 * cb + 1, :, r0:r0 + CONV_ROWS] = vt[MLSTM_QK_DIM:]
                else:
                    vb = (_silu(acc) * gain).astype(BF16)
                    dst_s[2 * cb, r0:r0 + CONV_ROWS, :] = vb[:, :MLSTM_QK_DIM]
                    dst_s[2 * cb + 1, r0:r0 + CONV_ROWS, :] = vb[:, MLSTM_QK_DIM:]

    ones_rows = jnp.ones((MLSTM_VA - MLSTM_V_DIM, CONV_ROWS), BF16)

    def vt_step(t, carry):
        r0 = pl.multiple_of(t * CONV_ROWS, CONV_ROWS)
        rows = pl.ds(r0, CONV_ROWS)
        for h in range(nh):
            vv = v_ref[0, rows, h * MLSTM_V_DIM:(h + 1) * MLSTM_V_DIM]
            vt_s[h, 0:MLSTM_V_DIM, rows] = vv.astype(F32).T.astype(BF16)
            vt_s[h, MLSTM_V_DIM:MLSTM_VA, rows] = ones_rows
        return carry

    lax.fori_loop(0, s // CONV_ROWS, vt_step, 0)

    def chunk_scores(h, d, ci):
        c = ci if d == 0 else nc - 1 - ci
        rows = slice(c * L, (c + 1) * L)
        sidx = d * nh + h
        state = st_s[sidx]
        qt_c = qt_s[h, :, rows]
        k_c = ks_s[h, rows, :]
        kq = _dot(jnp.concatenate([k_c, state.astype(BF16)], axis=0), qt_c)
        return rows, state, k_c, kq

    def chunk_finish(h, d, ci, rows, state, k_c, kq):
        c = ci if d == 0 else nc - 1 - ci
        sidx = d * nh + h
        jf = d * 2 * nh + nh + h
        ji = d * 2 * nh + h
        m = m_s[sidx]
        vt_c = vt_s[h, :, rows]
        ib_col = ibcol_s[:, c * 2 * nh + sidx:c * 2 * nh + sidx + 1]
        b_row = brow_s[jf:jf + 1, rows]
        i_row = gt_ref[0, 0, ji:ji + 1, rows] * LOG2E
        b_tot = b_row[:, L - 1:L] if d == 0 else b_row[:, 0:1]
        a_row = b_row + m
        dt = jnp.where(upp_b if d == 0 else low_b, b_row + ib_col, -jnp.inf)
        m_t = jnp.maximum(a_row, jnp.max(dt, axis=0, keepdims=True))
        w_inter = jnp.exp2(a_row - m_t)
        st = kq[:L] * jnp.exp2(dt - m_t)
        r = _dot(vt_c, st.astype(BF16)) + kq[L:] * w_inter
        den = r[MLSTM_V_DIM:MLSTM_V_DIM + 1]
        scale = 1.0 / jnp.maximum(jnp.abs(den), jnp.exp2(-m_t))
        ht_s[h, :, rows] += r[:MLSTM_V_DIM] * scale
        g_row = b_tot - b_row + i_row
        m_new = jnp.maximum(b_tot + m, jnp.max(g_row, axis=-1, keepdims=True))
        decay = jnp.exp2(b_tot + m - m_new)
        wk = jnp.exp2(g_row - m_new)
        vw = vt_c * wk.astype(BF16)
        st_s[sidx] = decay * state + _dot(vw, k_c)
        m_s[sidx] = m_new

    chains = [(h, d) for h in range(nh) for d in range(2)]
    for h, d in chains:
        st_s[d * nh + h] = jnp.zeros((MLSTM_VA, MLSTM_QK_DIM), F32)
        m_s[d * nh + h] = jnp.zeros((1, 1), F32)
    for h in range(nh):
        ht_s[h] = jnp.zeros((MLSTM_V_DIM, s), F32)

    for ci in range(nc):
        staged = [chunk_scores(h, d, ci) for h, d in chains]
        for (h, d), args in zip(chains, staged):
            chunk_finish(h, d, ci, *args)

    def fin(t, carry):
        r0 = pl.multiple_of(t * CONV_ROWS, CONV_ROWS)
        rows = pl.ds(r0, CONV_ROWS)
        hsts = [ht_s[h, :, rows] for h in range(nh)]
        hns = [(x * lax.rsqrt(jnp.mean(x * x, axis=0, keepdims=True) + EPS)).T for x in hsts]
        for h in range(nh):
            cols = slice(h * MLSTM_V_DIM, (h + 1) * MLSTM_V_DIM)
            og = _sigmoid(o_ref[0, rows, cols].astype(F32))
            y_ref[0, rows, cols] = (og * (hns[h] * gain_ref[:, cols])).astype(BF16)
        return carry

    lax.fori_loop(0, s // CONV_ROWS, fin, 0)


def _mlstm_call(q, k, v, o, gt, conv_w, conv_b, head_gain):
    b, s, _ = q.shape
    n_hg = MLSTM_HEADS // MLSTM_HPG
    qw = MLSTM_HPG * MLSTM_QK_DIM
    vw = MLSTM_HPG * MLSTM_V_DIM
    gt = gt.reshape(b, n_hg, GATES_PG, s)
    return pl.pallas_call(
        _mlstm_kernel,
        out_shape=jax.ShapeDtypeStruct((b, s, MLSTM_V_WIDTH), BF16),
        grid=(b, n_hg),
        in_specs=[
            pl.BlockSpec((1, s, qw), lambda i, g: (i, 0, g)),
            pl.BlockSpec((1, s, qw), lambda i, g: (i, 0, g)),
            pl.BlockSpec((1, s, vw), lambda i, g: (i, 0, g)),
            pl.BlockSpec((1, s, vw), lambda i, g: (i, 0, g)),
            pl.BlockSpec((1, 1, GATES_PG, s), lambda i, g: (i, g, 0, 0)),
            pl.BlockSpec((CONV_K, qw), lambda i, g: (0, g)),
            pl.BlockSpec((CONV_K, qw), lambda i, g: (0, n_hg + g)),
            pl.BlockSpec((1, qw), lambda i, g: (0, g)),
            pl.BlockSpec((1, qw), lambda i, g: (0, n_hg + g)),
            pl.BlockSpec((1, vw), lambda i, g: (0, g)),
        ],
        out_specs=pl.BlockSpec((1, s, vw), lambda i, g: (i, 0, g)),
        scratch_shapes=[
            pltpu.VMEM((s + 2 * CONV_PAD, LANES), F32),
            pltpu.VMEM((MLSTM_HPG, MLSTM_QK_DIM, s), BF16),
            pltpu.VMEM((MLSTM_HPG, s, MLSTM_QK_DIM), BF16),
            pltpu.VMEM((MLSTM_HPG, MLSTM_VA, s), BF16),
            pltpu.VMEM((MLSTM_CHUNK, (s // MLSTM_CHUNK) * 2 * MLSTM_HPG), F32),
            pltpu.VMEM((GATES_PG, s), F32),
            pltpu.VMEM((2 * MLSTM_HPG, MLSTM_VA, MLSTM_QK_DIM), F32),
            pltpu.VMEM((2 * MLSTM_HPG, 1, 1), F32),
            pltpu.VMEM((MLSTM_HPG, MLSTM_V_DIM, s), F32),
        ],
        compiler_params=_cparams("parallel", "parallel"),
        name="mlstm_mixer",
    )(q, k, v, o, gt, conv_w, conv_w, conv_b, conv_b, head_gain)


FFN_SLABS = ((0, 1280), (1280, 2816))
POST_SPLIT = 2


def _post_kernel(x_ref, y_ref, mod_ref, wo_ref, w1_ref, w3_ref, w2_ref, o_ref):
    m = mod_ref[0]
    g1, sh2, sc2, g2 = m[2:3], m[3:4], m[4:5], m[5:6]
    tm = x_ref.shape[1]
    halves = [slice(i * (tm // POST_SPLIT), (i + 1) * (tm // POST_SPLIT)) for i in range(POST_SPLIT)]
    x1s = [x_ref[0, r, :] + g1 * _dot(y_ref[0, r, :], wo_ref[...]) for r in halves]
    hs = [_norm_mod(x1, sh2, sc2).astype(BF16) for x1 in x1s]
    accs = [None] * POST_SPLIT
    for lo, hi in FFN_SLABS:
        gacts = []
        for h in hs:
            a = _dot(h, w1_ref[:, lo:hi])
            b = _dot(h, w3_ref[:, lo:hi])
            gacts.append((_silu(a) * b).astype(BF16))
        for i, gact in enumerate(gacts):
            contrib = _dot(gact, w2_ref[lo:hi, :])
            accs[i] = contrib if accs[i] is None else accs[i] + contrib
    for r, x1, acc in zip(halves, x1s, accs):
        o_ref[0, r, :] = x1 + g2 * acc


def _post_call(x, y, mod, w_out, w1, w3, w2):
    b, s, d = x.shape
    tm = min(512, s)
    const2 = lambda i, j: (0, 0)
    assert FFN_SLABS[-1][1] == w1.shape[1]
    return pl.pallas_call(
        _post_kernel,
        out_shape=jax.ShapeDtypeStruct((b, s, d), F32),
        grid=(b, s // tm),
        in_specs=[
            pl.BlockSpec((1, tm, d), lambda i, j: (i, j, 0)),
            pl.BlockSpec((1, tm, y.shape[2]), lambda i, j: (i, j, 0)),
            pl.BlockSpec((1, 6, d), lambda i, j: (i, 0, 0)),
            pl.BlockSpec(w_out.shape, lambda i, j: (0, 0)),
            pl.BlockSpec(w1.shape, const2, pipeline_mode=pl.Buffered(1)),
            pl.BlockSpec(w3.shape, const2, pipeline_mode=pl.Buffered(1)),
            pl.BlockSpec(w2.shape, const2, pipeline_mode=pl.Buffered(1)),
        ],
        out_specs=pl.BlockSpec((1, tm, d), lambda i, j: (i, j, 0)),
        compiler_params=_cparams("parallel", "parallel"),
        name="post_ffn",
    )(x, y, mod, w_out, w1, w3, w2)


def _rope_tables(s):
    rows = s // GRID_W
    row = jnp.repeat(jnp.arange(rows), GRID_W).astype(F32)
    col = jnp.tile(jnp.arange(GRID_W), rows).astype(F32)
    n_freq = ATT_HEAD_DIM // 4
    inv_freq = ROPE_THETA ** (-jnp.arange(n_freq, dtype=F32) / n_freq)
    ang = jnp.concatenate([row[:, None] * inv_freq, col[:, None] * inv_freq], axis=-1)
    cos, sin = jnp.cos(ang), jnp.sin(ang)
    cos_t = jnp.tile(cos, (1, LANES // cos.shape[1]))
    sin_t = jnp.tile(jnp.concatenate([-sin, sin], axis=-1), (1, LANES // ATT_HEAD_DIM))
    return cos_t, sin_t


def _pool_band():
    r = np.arange(POOL_TILE)[:, None]
    c = np.arange(POOL_TILE + 2 * POOL_HALO)[None, :]
    bands = []
    for w in POOL_WINDOWS:
        off = c - POOL_HALO - r + w // 2
        bands.append((off >= 0) & (off < w))
    return jnp.asarray(np.stack(bands), dtype=BF16)


def _head_split_perm():
    return np.concatenate([np.arange(0, ATT_HEAD_DIM, 2), np.arange(1, ATT_HEAD_DIM, 2)])


def kernel(x, c, ada_w, ada_b, ev_w_in, ev_w_pool, ev_b_pool, ev_pool_scale, ev_q_gain, ev_k_gain,
           ev_w_out, od_w_in, od_conv_w, od_conv_b, od_gate_b, od_head_gain, od_w_out,
           ffn_w1, ffn_w3, ffn_w2):
    b, s, d = x.shape
    depth = ada_w.shape[0]
    mod = _ada_call(c, ada_w, ada_b).reshape(depth, b, 6, d)
    cos_t, sin_t = _rope_tables(s)
    band = _pool_band()
    perm = _head_split_perm()
    qk_cols = np.concatenate([POOL_DIM + hh * ATT_HEAD_DIM + perm for hh in range(ATT_HEADS + ATT_KV_HEADS)])
    col_perm = np.concatenate([np.arange(POOL_DIM), qk_cols, np.arange(POOL_DIM + ATT_Q_DIM + ATT_KV_DIM,
                                                                      POOL_DIM + ATT_Q_DIM + 2 * ATT_KV_DIM)])
    gate_perm = np.arange(N_GATES).reshape(2, 2, MLSTM_HEADS // MLSTM_HPG, MLSTM_HPG).transpose(2, 0, 1, 3).reshape(-1)

    for layer in range(depth):
        j = layer // 2
        if layer % 2 == 0:
            w_in = ev_w_in[j][:, col_perm].astype(BF16)
            u, q, kv = _inproj_even_call(x, mod[layer], w_in)
            qg = jnp.tile(ev_q_gain[j][perm], LANES // ATT_HEAD_DIM).reshape(1, LANES)
            kg = jnp.tile(ev_k_gain[j][perm], LANES // ATT_HEAD_DIM).reshape(1, LANES)
            y = _even_mixer_call(u, q, kv, cos_t, sin_t, qg, kg, band, ev_w_pool[j].astype(BF16),
                                 ev_b_pool[j].reshape(-1, 1, POOL_GROUP_DIM),
                                 ev_pool_scale[j].reshape(-1, 1, POOL_GROUP_DIM))
            w_out = ev_w_out[j].astype(BF16)
        else:
            n_main = 2 * MLSTM_QK_WIDTH + 2 * MLSTM_V_WIDTH
            w_in = od_w_in[j][:, :n_main].astype(BF16)
            wgt = od_w_in[j][:, n_main:][:, gate_perm].T.astype(BF16)
            q, k, v, o, gt = _inproj_odd_call(x, mod[layer], w_in, wgt, od_gate_b[j][gate_perm])
            y = _mlstm_call(q, k, v, o, gt, od_conv_w[j], od_conv_b[j].reshape(1, -1),
                            od_head_gain[j].reshape(1, -1))
            w_out = od_w_out[j].astype(BF16)
        x = _post_call(x, y, mod[layer], w_out, ffn_w1[layer].astype(BF16), ffn_w3[layer].astype(BF16),
                       ffn_w2[layer].astype(BF16))
    return x
```

```python
import jax
import jax.numpy as jnp
import numpy as np
from jax import lax
from jax.experimental import pallas as pl
from jax.experimental.pallas import tpu as pltpu

F32 = jnp.float32
BF16 = jnp.bfloat16

EPS = 1e-6
GRID_W = 64
ROPE_THETA = 10000.0
POOL_WINDOWS = (2, 4, 8, 16)
POOL_GROUP_DIM = 128
POOL_DIM = 512
ATT_HEADS = 8
ATT_KV_HEADS = 2
ATT_GROUP = ATT_HEADS // ATT_KV_HEADS
ATT_HEAD_DIM = 64
ATT_Q_DIM = 512
ATT_KV_DIM = 128
MLSTM_HEADS = 8
MLSTM_QK_DIM = 64
MLSTM_V_DIM = 128
MLSTM_QK_WIDTH = 512
MLSTM_V_WIDTH = 1024
N_GATES = 32
CONV_K = 5

LANES = 128
POOL_HALO = 16
POOL_TILE = 256
CONV_PAD = 8
VMEM_LIMIT = 56 * 1024 * 1024


def _cparams(*sem):
    return pltpu.CompilerParams(dimension_semantics=sem, vmem_limit_bytes=VMEM_LIMIT)


def _sigmoid(x):
    return 1.0 / (1.0 + jnp.exp(-x))


def _silu(x):
    return x * _sigmoid(x)


def _log_sigmoid(x):
    return jnp.minimum(x, 0.0) - jnp.log(1.0 + jnp.exp(-jnp.abs(x)))


def _norm_mod(x, shift, scale):
    ms = jnp.mean(x * x, axis=-1, keepdims=True)
    return (x * lax.rsqrt(ms + EPS)) * (1.0 + scale) + shift


def _split3(x):
    x1 = x.astype(BF16)
    r1 = x - x1.astype(F32)
    x2 = r1.astype(BF16)
    x3 = (r1 - x2.astype(F32)).astype(BF16)
    return x1, x2, x3


def _dot(a, b):
    return jnp.dot(a, b, preferred_element_type=F32)


def _dot_nt(a, b):
    return lax.dot_general(a, b, (((1,), (1,)), ((), ())), preferred_element_type=F32)


def _ada_kernel(c_ref, w_ref, b_ref, o_ref):
    cond = _silu(c_ref[...]).astype(BF16)
    o_ref[0] = _dot(cond, w_ref[0].astype(BF16)) + b_ref[0]


def _ada_call(c, ada_w, ada_b):
    depth, d, n = ada_w.shape
    b = c.shape[0]
    tn = 1024
    return pl.pallas_call(
        _ada_kernel,
        out_shape=jax.ShapeDtypeStruct((depth, b, n), F32),
        grid=(depth, n // tn),
        in_specs=[
            pl.BlockSpec((b, d), lambda l, j: (0, 0)),
            pl.BlockSpec((1, d, tn), lambda l, j: (l, 0, j)),
            pl.BlockSpec((1, 1, tn), lambda l, j: (l, 0, j)),
        ],
        out_specs=pl.BlockSpec((1, b, tn), lambda l, j: (l, 0, j)),
        compiler_params=_cparams("arbitrary", "arbitrary"),
        name="ada_table",
    )(c, ada_w, ada_b.reshape(depth, 1, n))


INPROJ_ROWS = 512


def _row_groups(tm):
    return [slice(r, r + INPROJ_ROWS) for r in range(0, tm, INPROJ_ROWS)]


def _inproj_even_kernel(x_ref, mod_ref, w_ref, u_ref, q_ref, kv_ref):
    m = mod_ref[0]
    groups = _row_groups(x_ref.shape[1])
    hs = [_norm_mod(x_ref[0, r, :], m[0:1], m[1:2]).astype(BF16) for r in groups]
    for r, h in zip(groups, hs):
        z = _dot(h, w_ref[...]).astype(BF16)
        u_ref[0, r, :] = z[:, :POOL_DIM]
        q_ref[0, r, :] = z[:, POOL_DIM:POOL_DIM + ATT_Q_DIM]
        kv_ref[0, r, :] = z[:, POOL_DIM + ATT_Q_DIM:]


def _inproj_even_call(x, mod, w):
    b, s, d = x.shape
    n = w.shape[1]
    tm = min(2048, s)
    widths = (POOL_DIM, ATT_Q_DIM, 2 * ATT_KV_DIM)
    return pl.pallas_call(
        _inproj_even_kernel,
        out_shape=tuple(jax.ShapeDtypeStruct((b, s, wd), BF16) for wd in widths),
        grid=(b, s // tm),
        in_specs=[
            pl.BlockSpec((1, tm, d), lambda i, j: (i, j, 0)),
            pl.BlockSpec((1, 6, d), lambda i, j: (i, 0, 0)),
            pl.BlockSpec((d, n), lambda i, j: (0, 0)),
        ],
        out_specs=tuple(pl.BlockSpec((1, tm, wd), lambda i, j: (i, j, 0)) for wd in widths),
        compiler_params=_cparams("parallel", "parallel"),
        name="inproj_even",
    )(x, mod, w)


def _inproj_odd_kernel(x_ref, mod_ref, w_ref, wvt_ref, wgt_ref, gbt_ref, q_ref, k_ref, o_ref, vt_ref, gt_ref):
    m = mod_ref[0]
    groups = _row_groups(x_ref.shape[1])
    hs = [_norm_mod(x_ref[0, r, :], m[0:1], m[1:2]).astype(BF16) for r in groups]
    for r, h in zip(groups, hs):
        z = _dot(h, w_ref[...]).astype(BF16)
        q_ref[0, r, :] = z[:, :MLSTM_QK_WIDTH]
        k_ref[0, r, :] = z[:, MLSTM_QK_WIDTH:2 * MLSTM_QK_WIDTH]
        o_ref[0, r, :] = z[:, 2 * MLSTM_QK_WIDTH:]
        vt_ref[0, :, r] = _dot_nt(wvt_ref[...], h).astype(BF16)
        gt_ref[0, :, r] = _dot_nt(wgt_ref[...], h) + gbt_ref[...]


def _inproj_odd_call(x, mod, w, wvt, wgt, gate_b):
    b, s, d = x.shape
    n = w.shape[1]
    tm = min(1024, s)
    widths = (MLSTM_QK_WIDTH, MLSTM_QK_WIDTH, MLSTM_V_WIDTH)
    return pl.pallas_call(
        _inproj_odd_kernel,
        out_shape=tuple(jax.ShapeDtypeStruct((b, s, wd), BF16) for wd in widths)
        + (jax.ShapeDtypeStruct((b, MLSTM_V_WIDTH, s), BF16), jax.ShapeDtypeStruct((b, N_GATES, s), F32)),
        grid=(b, s // tm),
        in_specs=[
            pl.BlockSpec((1, tm, d), lambda i, j: (i, j, 0)),
            pl.BlockSpec((1, 6, d), lambda i, j: (i, 0, 0)),
            pl.BlockSpec((d, n), lambda i, j: (0, 0)),
            pl.BlockSpec((MLSTM_V_WIDTH, d), lambda i, j: (0, 0)),
            pl.BlockSpec((N_GATES, d), lambda i, j: (0, 0)),
            pl.BlockSpec((N_GATES, 1), lambda i, j: (0, 0)),
        ],
        out_specs=tuple(pl.BlockSpec((1, tm, wd), lambda i, j: (i, j, 0)) for wd in widths)
        + (pl.BlockSpec((1, MLSTM_V_WIDTH, tm), lambda i, j: (i, 0, j)),
           pl.BlockSpec((1, N_GATES, tm), lambda i, j: (i, 0, j))),
        compiler_params=_cparams("parallel", "parallel"),
        name="inproj_odd",
    )(x, mod, w, wvt, wgt, gate_b.reshape(N_GATES, 1))


ATT_QB = 256
ATT_PREP = 256
LOG2E = 1.4426950408889634
ATT_UNROLL = 8
ATT_KB = 256
ATT_VA = ATT_HEAD_DIM + 16


def _seg_mean_sq(x, bd):
    x2 = x * x
    hi = x2.astype(BF16)
    lo = (x2 - hi.astype(F32)).astype(BF16)
    return (_dot(hi, bd) + _dot(lo, bd)) * (1.0 / ATT_HEAD_DIM)


def _rope_swap(x, first_half):
    return jnp.where(first_half, pltpu.roll(x, LANES - 32, 1), pltpu.roll(x, 32, 1))


def _even_mixer_kernel(u_ref, q_ref, kv_ref, cos_ref, sin_ref, qg_ref, kg_ref, band_ref,
                       wp_ref, bp_ref, ps_ref, y_ref, up_s, qt_s, ks_s, vt_s, st_s, mx_s):
    s = u_ref.shape[1]
    lane = lax.broadcasted_iota(jnp.int32, (1, LANES), 1)
    first_half = (lane % ATT_HEAD_DIM) < (ATT_HEAD_DIM // 2)
    r_i = lax.broadcasted_iota(jnp.int32, (LANES, LANES), 0)
    c_i = lax.broadcasted_iota(jnp.int32, (LANES, LANES), 1)
    bd = jnp.where((r_i // ATT_HEAD_DIM) == (c_i // ATT_HEAD_DIM), 1.0, 0.0).astype(BF16)

    def prep(t, carry):
        r0 = pl.multiple_of(t * ATT_PREP, ATT_PREP)
        rows = pl.ds(r0, ATT_PREP)
        cos = cos_ref[rows, :]
        sin = sin_ref[rows, :]

        n_qc = ATT_Q_DIM // LANES
        xs = [q_ref[0, rows, cb * LANES:(cb + 1) * LANES].astype(F32) for cb in range(n_qc)]
        xs.append(kv_ref[0, rows, 0:LANES].astype(F32))
        gains = [qg_ref[...]] * n_qc + [kg_ref[...]]
        mss = [_seg_mean_sq(xb, bd) for xb in xs]
        roped = []
        for xb, ms, gain in zip(xs, mss, gains):
            xn = xb * lax.rsqrt(ms + EPS) * gain
            roped.append(xn * cos + _rope_swap(xn, first_half) * sin)
        for cb in range(n_qc):
            qt = (roped[cb] * (ATT_HEAD_DIM ** -0.5 * LOG2E)).T.astype(BF16)
            qt_s[2 * cb, :, rows] = qt[:ATT_HEAD_DIM]
            qt_s[2 * cb + 1, :, rows] = qt[ATT_HEAD_DIM:]
        kn = roped[n_qc].astype(BF16)
        ks_s[0, rows, :] = kn[:, :ATT_HEAD_DIM]
        ks_s[1, rows, :] = kn[:, ATT_HEAD_DIM:]
        vt = kv_ref[0, rows, LANES:2 * LANES].astype(F32).T.astype(BF16)
        ones = jnp.ones((ATT_VA - ATT_HEAD_DIM, ATT_PREP), BF16)
        vt_s[0, :, rows] = jnp.concatenate([vt[:ATT_HEAD_DIM], ones], axis=0)
        vt_s[1, :, rows] = jnp.concatenate([vt[ATT_HEAD_DIM:], ones], axis=0)
        return carry

    lax.fori_loop(0, s // ATT_PREP, prep, 0)

    n_qb = s // ATT_QB
    n_steps = ATT_KV_HEADS * (ATT_GROUP // 2) * n_qb

    def decode(t):
        j = t // ((ATT_GROUP // 2) * n_qb)
        gp = (t // n_qb) % (ATT_GROUP // 2)
        rows = pl.ds(pl.multiple_of((t % n_qb) * ATT_QB, ATT_QB), ATT_QB)
        return j, ATT_GROUP * j + 2 * gp, rows

    n_kb = s // ATT_KB

    def att_step(t_next, t_cur, slot_next, slot_cur):
        jn, hn, rows_n = decode(t_next)
        jc, hc, rows_c = decode(t_cur)
        qt2 = jnp.concatenate([qt_s[hn, :, rows_n], qt_s[hn + 1, :, rows_n]], axis=1)
        mx_cur = mx_s[slot_cur]
        mx_next = None
        acc = [None, None]
        for kb in range(n_kb):
            keys = slice(kb * ATT_KB, (kb + 1) * ATT_KB)
            st = _dot(ks_s[jn, keys, :], qt2)
            st_s[slot_next, keys, :] = st
            bmx = jnp.max(st, axis=0, keepdims=True)
            mx_next = bmx if mx_next is None else jnp.maximum(mx_next, bmx)
            p = jnp.exp2(st_s[slot_cur, keys, :] - mx_cur).astype(BF16)
            for g in range(2):
                part = _dot(vt_s[jc, :, keys], p[:, g * ATT_QB:(g + 1) * ATT_QB])
                acc[g] = part if acc[g] is None else acc[g] + part
        mx_s[slot_next] = mx_next
        outs = [a[:ATT_HEAD_DIM] / a[ATT_HEAD_DIM:ATT_HEAD_DIM + 1] for a in acc]
        pair = jnp.concatenate(outs, axis=0).T
        c0 = pl.multiple_of(POOL_DIM + hc * ATT_HEAD_DIM, LANES)
        y_ref[0, rows_c, pl.ds(c0, LANES)] = pair.astype(BF16)

    j0, h00, rows0 = decode(0)
    st0 = _dot(ks_s[j0], jnp.concatenate([qt_s[h00, :, rows0], qt_s[h00 + 1, :, rows0]], axis=1))
    st_s[0] = st0
    mx_s[0] = jnp.max(st0, axis=0, keepdims=True)

    def att_body(k, carry):
        t0 = ATT_UNROLL * k
        for u in range(ATT_UNROLL):
            att_step(jnp.minimum(t0 + u + 1, n_steps - 1), t0 + u, (u + 1) % 2, u % 2)
        return carry

    lax.fori_loop(0, n_steps // ATT_UNROLL, att_body, 0)

    zeros_h = jnp.zeros((POOL_HALO, POOL_DIM), BF16)
    up_s[0:POOL_HALO, :] = zeros_h
    up_s[POOL_HALO + s:2 * POOL_HALO + s, :] = zeros_h
    up_s[POOL_HALO:POOL_HALO + s, :] = u_ref[0]

    def pool_step(t, carry):
        r0 = pl.multiple_of(t * POOL_TILE, POOL_TILE)
        tok = r0 + lax.broadcasted_iota(jnp.int32, (POOL_TILE, 1), 0)
        groups = list(enumerate(POOL_WINDOWS))
        cols = [slice(g * POOL_GROUP_DIM, (g + 1) * POOL_GROUP_DIM) for g, _ in groups]
        tots = [_dot(band_ref[g], up_s[pl.ds(r0, POOL_TILE + 2 * POOL_HALO), cols[g]]) for g, _ in groups]
        pooled = []
        for g, w in groups:
            lo = jnp.maximum(tok - w // 2, 0)
            hi = jnp.minimum(tok + (w - w // 2), s)
            cnt = (hi - lo).astype(F32)
            ug = u_ref[0, pl.ds(r0, POOL_TILE), cols[g]].astype(F32)
            pooled.append((tots[g] / cnt - ug).astype(BF16))
        proj = [_dot(pooled[g], wp_ref[g]) for g, _ in groups]
        for g, _ in groups:
            y_ref[0, pl.ds(r0, POOL_TILE), cols[g]] = ((proj[g] + bp_ref[g]) * ps_ref[g]).astype(BF16)
        return carry

    lax.fori_loop(0, s // POOL_TILE, pool_step, 0)


def _even_mixer_call(u, q, kv, cos_t, sin_t, q_gain, k_gain, band, w_pool, b_pool, pool_scale):
    b, s, _ = u.shape
    return pl.pallas_call(
        _even_mixer_kernel,
        out_shape=jax.ShapeDtypeStruct((b, s, POOL_DIM + ATT_Q_DIM), BF16),
        grid=(b,),
        in_specs=[
            pl.BlockSpec((1, s, POOL_DIM), lambda i: (i, 0, 0)),
            pl.BlockSpec((1, s, ATT_Q_DIM), lambda i: (i, 0, 0)),
            pl.BlockSpec((1, s, 2 * ATT_KV_DIM), lambda i: (i, 0, 0)),
            pl.BlockSpec((s, LANES), lambda i: (0, 0)),
            pl.BlockSpec((s, LANES), lambda i: (0, 0)),
            pl.BlockSpec((1, LANES), lambda i: (0, 0)),
            pl.BlockSpec((1, LANES), lambda i: (0, 0)),
            pl.BlockSpec(band.shape, lambda i: (0, 0, 0)),
            pl.BlockSpec(w_pool.shape, lambda i: (0, 0, 0)),
            pl.BlockSpec(b_pool.shape, lambda i: (0, 0, 0)),
            pl.BlockSpec(pool_scale.shape, lambda i: (0, 0, 0)),
        ],
        out_specs=pl.BlockSpec((1, s, POOL_DIM + ATT_Q_DIM), lambda i: (i, 0, 0)),
        scratch_shapes=[
            pltpu.VMEM((s + 2 * POOL_HALO, POOL_DIM), BF16),
            pltpu.VMEM((ATT_HEADS, ATT_HEAD_DIM, s), BF16),
            pltpu.VMEM((ATT_KV_HEADS, s, ATT_HEAD_DIM), BF16),
            pltpu.VMEM((ATT_KV_HEADS, ATT_VA, s), BF16),
            pltpu.VMEM((2, s, 2 * ATT_QB), F32),
            pltpu.VMEM((2, 1, 2 * ATT_QB), F32),
        ],
        compiler_params=_cparams("parallel"),
        name="even_mixer",
    )(u, q, kv, cos_t, sin_t, q_gain, k_gain, band, w_pool, b_pool, pool_scale)


CONV_ROWS = 256
MLSTM_HPG = 4
MLSTM_VA = MLSTM_V_DIM + 16
MLSTM_CHUNK = 256
GATES_PG = 4 * MLSTM_HPG


def _mlstm_kernel(q_ref, k_ref, vt_ref, o_ref, gt_ref, cwq_ref, cwk_ref, cbq_ref, cbk_ref, gain_ref, y_ref,
                  pad_s, qt_s, ks_s, ibcol_s, brow_s, st_s, m_s, ht_s):
    s = q_ref.shape[1]
    L = MLSTM_CHUNK
    nc = s // L
    nh = MLSTM_HPG
    r_i = lax.broadcasted_iota(jnp.int32, (L, L), 0)
    c_i = lax.broadcasted_iota(jnp.int32, (L, L), 1)
    low_b = r_i >= c_i
    upp_b = c_i >= r_i
    low01 = jnp.where(low_b, 1.0, 0.0).astype(BF16)
    upp01 = jnp.where(upp_b, 1.0, 0.0).astype(BF16)
    eye01 = jnp.where(r_i == c_i, 1.0, 0.0).astype(BF16)

    def stack(x):
        return jnp.concatenate([x[:, c * L:(c + 1) * L] for c in range(nc)], axis=0)

    gt = gt_ref[0, 0]
    lf3 = jnp.concatenate(_split3(stack(_log_sigmoid(gt) * LOG2E)), axis=0)
    n_st = nc * GATES_PG
    pre = _dot(lf3, upp01)
    suf = _dot(lf3, low01)
    pre = pre[:n_st] + pre[n_st:2 * n_st] + pre[2 * n_st:]
    suf = suf[:n_st] + suf[n_st:2 * n_st] + suf[2 * n_st:]
    fwd_row = (lax.broadcasted_iota(jnp.int32, (n_st, 1), 0) % GATES_PG) < GATES_PG // 2
    bq = jnp.where(fwd_row, pre, suf)
    i2 = stack(gt * LOG2E)
    ib = []
    for c in range(nc):
        r0 = c * GATES_PG
        brow_s[:, c * L:(c + 1) * L] = bq[r0:r0 + GATES_PG]
        ib.append(i2[r0:r0 + nh] - bq[r0 + nh:r0 + 2 * nh])
        ib.append(i2[r0 + 2 * nh:r0 + 3 * nh] - bq[r0 + 3 * nh:r0 + 4 * nh])
    ib3 = jnp.concatenate(_split3(jnp.concatenate(ib, axis=0)), axis=0)
    ibt = _dot_nt(eye01, ib3)
    n_ib = nc * 2 * nh
    ibcol_s[...] = ibt[:, :n_ib] + ibt[:, n_ib:2 * n_ib] + ibt[:, 2 * n_ib:]

    zpad = jnp.zeros((CONV_PAD, LANES), F32)
    pad_s[0:CONV_PAD, :] = zpad
    pad_s[CONV_PAD + s:2 * CONV_PAD + s, :] = zpad
    for src_ref, cw_ref, cb_ref, dst_s, gain, transposed in (
            (q_ref, cwq_ref, cbq_ref, qt_s, 1.0, True),
            (k_ref, cwk_ref, cbk_ref, ks_s, MLSTM_QK_DIM ** -0.5, False)):
        for cb in range(nh * MLSTM_QK_DIM // LANES):
            cols = slice(cb * LANES, (cb + 1) * LANES)
            pad_s[CONV_PAD:CONV_PAD + s, :] = src_ref[0, :, cols].astype(F32)
            cw = cw_ref[:, cols]
            cbias = cb_ref[:, cols]
            for t in range(s // CONV_ROWS):
                r0 = t * CONV_ROWS
                acc = cbias + cw[0:1] * pad_s[r0 + CONV_PAD - 2:r0 + CONV_PAD - 2 + CONV_ROWS, :]
                for j in range(1, CONV_K):
                    off = r0 + CONV_PAD - CONV_K // 2 + j
                    acc = acc + cw[j:j + 1] * pad_s[off:off + CONV_ROWS, :]
                if transposed:
                    vt = _silu(acc).T.astype(BF16)
                    dst_s[2 * cb, :, r0:r0 + CONV_ROWS] = vt[:MLSTM_QK_DIM]
                    dst_s[2 * cb + 1, :, r0:r0 + CONV_ROWS] = vt[MLSTM_QK_DIM:]
                else:
                    vb = (_silu(acc) * gain).astype(BF16)
                    dst_s[2 * cb, r0:r0 + CONV_ROWS, :] = vb[:, :MLSTM_QK_DIM]
                    dst_s[2 * cb + 1, r0:r0 + CONV_ROWS, :] = vb[:, MLSTM_QK_DIM:]

    ones_rows = jnp.ones((MLSTM_VA - MLSTM_V_DIM, L), BF16)

    def chunk_scores(h, d, ci):
        c = ci if d == 0 else nc - 1 - ci
        rows = slice(c * L, (c + 1) * L)
        sidx = d * nh + h
        state = st_s[sidx]
        qt_c = qt_s[h, :, rows]
        k_c = ks_s[h, rows, :]
        kq = _dot(jnp.concatenate([k_c, state.astype(BF16)], axis=0), qt_c)
        return rows, state, k_c, kq

    def chunk_finish(h, d, ci, rows, state, k_c, kq):
        c = ci if d == 0 else nc - 1 - ci
        sidx = d * nh + h
        jf = d * 2 * nh + nh + h
        ji = d * 2 * nh + h
        m = m_s[sidx]
        vt_c = jnp.concatenate([vt_ref[0, h * MLSTM_V_DIM:(h + 1) * MLSTM_V_DIM, rows], ones_rows], axis=0)
        ib_col = ibcol_s[:, c * 2 * nh + sidx:c * 2 * nh + sidx + 1]
        b_row = brow_s[jf:jf + 1, rows]
        i_row = gt_ref[0, 0, ji:ji + 1, rows] * LOG2E
        b_tot = b_row[:, L - 1:L] if d == 0 else b_row[:, 0:1]
        a_row = b_row + m
        dt = jnp.where(upp_b if d == 0 else low_b, b_row + ib_col, -jnp.inf)
        m_t = jnp.maximum(a_row, jnp.max(dt, axis=0, keepdims=True))
        w_inter = jnp.exp2(a_row - m_t)
        st = kq[:L] * jnp.exp2(dt - m_t)
        r = _dot(vt_c, st.astype(BF16)) + kq[L:] * w_inter
        den = r[MLSTM_V_DIM:MLSTM_V_DIM + 1]
        scale = 1.0 / jnp.maximum(jnp.abs(den), jnp.exp2(-m_t))
        ht_s[h, :, rows] += r[:MLSTM_V_DIM] * scale
        g_row = b_tot - b_row + i_row
        m_new = jnp.maximum(b_tot + m, jnp.max(g_row, axis=-1, keepdims=True))
        decay = jnp.exp2(b_tot + m - m_new)
        wk = jnp.exp2(g_row - m_new)
        vw = vt_c * wk.astype(BF16)
        st_s[sidx] = decay * state + _dot(vw, k_c)
        m_s[sidx] = m_new

    chains = [(h, d) for h in range(nh) for d in range(2)]
    for h, d in chains:
        st_s[d * nh + h] = jnp.zeros((MLSTM_VA, MLSTM_QK_DIM), F32)
        m_s[d * nh + h] = jnp.zeros((1, 1), F32)
    for h in range(nh):
        ht_s[h] = jnp.zeros((MLSTM_V_DIM, s), F32)

    for ci in range(nc):
        staged = [chunk_scores(h, d, ci) for h, d in chains]
        for (h, d), args in zip(chains, staged):
            chunk_finish(h, d, ci, *args)

    def fin(t, carry):
        r0 = pl.multiple_of(t * CONV_ROWS, CONV_ROWS)
        rows = pl.ds(r0, CONV_ROWS)
        hsts = [ht_s[h, :, rows] for h in range(nh)]
        hns = [(x * lax.rsqrt(jnp.mean(x * x, axis=0, keepdims=True) + EPS)).T for x in hsts]
        for h in range(nh):
            cols = slice(h * MLSTM_V_DIM, (h + 1) * MLSTM_V_DIM)
            og = _sigmoid(o_ref[0, rows, cols].astype(F32))
            y_ref[0, rows, cols] = (og * (hns[h] * gain_ref[:, cols])).astype(BF16)
        return carry

    lax.fori_loop(0, s // CONV_ROWS, fin, 0)


def _mlstm_call(q, k, vt, o, gt, conv_w, conv_b, head_gain):
    b, s, _ = q.shape
    n_hg = MLSTM_HEADS // MLSTM_HPG
    qw = MLSTM_HPG * MLSTM_QK_DIM
    vw = MLSTM_HPG * MLSTM_V_DIM
    gt = gt.reshape(b, n_hg, GATES_PG, s)
    return pl.pallas_call(
        _mlstm_kernel,
        out_shape=jax.ShapeDtypeStruct((b, s, MLSTM_V_WIDTH), BF16),
        grid=(b, n_hg),
        in_specs=[
            pl.BlockSpec((1, s, qw), lambda i, g: (i, 0, g)),
            pl.BlockSpec((1, s, qw), lambda i, g: (i, 0, g)),
            pl.BlockSpec((1, vw, s), lambda i, g: (i, g, 0)),
            pl.BlockSpec((1, s, vw), lambda i, g: (i, 0, g)),
            pl.BlockSpec((1, 1, GATES_PG, s), lambda i, g: (i, g, 0, 0)),
            pl.BlockSpec((CONV_K, qw), lambda i, g: (0, g)),
            pl.BlockSpec((CONV_K, qw), lambda i, g: (0, n_hg + g)),
            pl.BlockSpec((1, qw), lambda i, g: (0, g)),
            pl.BlockSpec((1, qw), lambda i, g: (0, n_hg + g)),
            pl.BlockSpec((1, vw), lambda i, g: (0, g)),
        ],
        out_specs=pl.BlockSpec((1, s, vw), lambda i, g: (i, 0, g)),
        scratch_shapes=[
            pltpu.VMEM((s + 2 * CONV_PAD, LANES), F32),
            pltpu.VMEM((MLSTM_HPG, MLSTM_QK_DIM, s), BF16),
            pltpu.VMEM((MLSTM_HPG, s, MLSTM_QK_DIM), BF16),
            pltpu.VMEM((MLSTM_CHUNK, (s // MLSTM_CHUNK) * 2 * MLSTM_HPG), F32),
            pltpu.VMEM((GATES_PG, s), F32),
            pltpu.VMEM((2 * MLSTM_HPG, MLSTM_VA, MLSTM_QK_DIM), F32),
            pltpu.VMEM((2 * MLSTM_HPG, 1, 1), F32),
            pltpu.VMEM((MLSTM_HPG, MLSTM_V_DIM, s), F32),
        ],
        compiler_params=_cparams("parallel", "parallel"),
        name="mlstm_mixer",
    )(q, k, vt, o, gt, conv_w, conv_w, conv_b, conv_b, head_gain)


FFN_SLABS = ((0, 1280), (1280, 2816))
POST_SPLIT = 2


def _post_kernel(x_ref, y_ref, mod_ref, wo_ref, w1_ref, w3_ref, w2_ref, o_ref):
    m = mod_ref[0]
    g1, sh2, sc2, g2 = m[2:3], m[3:4], m[4:5], m[5:6]
    tm = x_ref.shape[1]
    halves = [slice(i * (tm // POST_SPLIT), (i + 1) * (tm // POST_SPLIT)) for i in range(POST_SPLIT)]
    x1s = [x_ref[0, r, :] + g1 * _dot(y_ref[0, r, :], wo_ref[...]) for r in halves]
    hs = [_norm_mod(x1, sh2, sc2).astype(BF16) for x1 in x1s]
    accs = [None] * POST_SPLIT
    for lo, hi in FFN_SLABS:
        gacts = []
        for h in hs:
            a = _dot(h, w1_ref[:, lo:hi])
            b = _dot(h, w3_ref[:, lo:hi])
            gacts.append((_silu(a) * b).astype(BF16))
        for i, gact in enumerate(gacts):
            contrib = _dot(gact, w2_ref[lo:hi, :])
            accs[i] = contrib if accs[i] is None else accs[i] + contrib
    for r, x1, acc in zip(halves, x1s, accs):
        o_ref[0, r, :] = x1 + g2 * acc


def _post_call(x, y, mod, w_out, w1, w3, w2):
    b, s, d = x.shape
    tm = min(512, s)
    const2 = lambda i, j: (0, 0)
    assert FFN_SLABS[-1][1] == w1.shape[1]
    return pl.pallas_call(
        _post_kernel,
        out_shape=jax.ShapeDtypeStruct((b, s, d), F32),
        grid=(b, s // tm),
        in_specs=[
            pl.BlockSpec((1, tm, d), lambda i, j: (i, j, 0)),
            pl.BlockSpec((1, tm, y.shape[2]), lambda i, j: (i, j, 0)),
            pl.BlockSpec((1, 6, d), lambda i, j: (i, 0, 0)),
            pl.BlockSpec(w_out.shape, lambda i, j: (0, 0)),
            pl.BlockSpec(w1.shape, const2, pipeline_mode=pl.Buffered(1)),
            pl.BlockSpec(w3.shape, const2, pipeline_mode=pl.Buffered(1)),
            pl.BlockSpec(w2.shape, const2, pipeline_mode=pl.Buffered(1)),
        ],
        out_specs=pl.BlockSpec((1, tm, d), lambda i, j: (i, j, 0)),
        compiler_params=_cparams("parallel", "parallel"),
        name="post_ffn",
    )(x, y, mod, w_out, w1, w3, w2)


def _rope_tables(s):
    rows = s // GRID_W
    row = jnp.repeat(jnp.arange(rows), GRID_W).astype(F32)
    col = jnp.tile(jnp.arange(GRID_W), rows).astype(F32)
    n_freq = ATT_HEAD_DIM // 4
    inv_freq = ROPE_THETA ** (-jnp.arange(n_freq, dtype=F32) / n_freq)
    ang = jnp.concatenate([row[:, None] * inv_freq, col[:, None] * inv_freq], axis=-1)
    cos, sin = jnp.cos(ang), jnp.sin(ang)
    cos_t = jnp.tile(cos, (1, LANES // cos.shape[1]))
    sin_t = jnp.tile(jnp.concatenate([-sin, sin], axis=-1), (1, LANES // ATT_HEAD_DIM))
    return cos_t, sin_t


def _pool_band():
    r = np.arange(POOL_TILE)[:, None]
    c = np.arange(POOL_TILE + 2 * POOL_HALO)[None, :]
    bands = []
    for w in POOL_WINDOWS:
        off = c - POOL_HALO - r + w // 2
        bands.append((off >= 0) & (off < w))
    return jnp.asarray(np.stack(bands), dtype=BF16)


def _head_split_perm():
    return np.concatenate([np.arange(0, ATT_HEAD_DIM, 2), np.arange(1, ATT_HEAD_DIM, 2)])


def kernel(x, c, ada_w, ada_b, ev_w_in, ev_w_pool, ev_b_pool, ev_pool_scale, ev_q_gain, ev_k_gain,
           ev_w_out, od_w_in, od_conv_w, od_conv_b, od_gate_b, od_head_gain, od_w_out,
           ffn_w1, ffn_w3, ffn_w2):
    b, s, d = x.shape
    depth = ada_w.shape[0]
    mod = _ada_call(c, ada_w, ada_b).reshape(depth, b, 6, d)
    cos_t, sin_t = _rope_tables(s)
    band = _pool_band()
    perm = _head_split_perm()
    qk_cols = np.concatenate([POOL_DIM + hh * ATT_HEAD_DIM + perm for hh in range(ATT_HEADS + ATT_KV_HEADS)])
    col_perm = np.concatenate([np.arange(POOL_DIM), qk_cols, np.arange(POOL_DIM + ATT_Q_DIM + ATT_KV_DIM,
                                                                      POOL_DIM + ATT_Q_DIM + 2 * ATT_KV_DIM)])
    gate_perm = np.arange(N_GATES).reshape(2, 2, MLSTM_HEADS // MLSTM_HPG, MLSTM_HPG).transpose(2, 0, 1, 3).reshape(-1)

    for layer in range(depth):
        j = layer // 2
        if layer % 2 == 0:
            w_in = ev_w_in[j][:, col_perm].astype(BF16)
            u, q, kv = _inproj_even_call(x, mod[layer], w_in)
            qg = jnp.tile(ev_q_gain[j][perm], LANES // ATT_HEAD_DIM).reshape(1, LANES)
            kg = jnp.tile(ev_k_gain[j][perm], LANES // ATT_HEAD_DIM).reshape(1, LANES)
            y = _even_mixer_call(u, q, kv, cos_t, sin_t, qg, kg, band, ev_w_pool[j].astype(BF16),
                                 ev_b_pool[j].reshape(-1, 1, POOL_GROUP_DIM),
                                 ev_pool_scale[j].reshape(-1, 1, POOL_GROUP_DIM))
            w_out = ev_w_out[j].astype(BF16)
        else:
            n_qk = 2 * MLSTM_QK_WIDTH
            n_main = n_qk + 2 * MLSTM_V_WIDTH
            w_qko = jnp.concatenate([od_w_in[j][:, :n_qk], od_w_in[j][:, n_qk + MLSTM_V_WIDTH:n_main]],
                                    axis=1).astype(BF16)
            wvt = od_w_in[j][:, n_qk:n_qk + MLSTM_V_WIDTH].T.astype(BF16)
            wgt = od_w_in[j][:, n_main:][:, gate_perm].T.astype(BF16)
            q, k, o, vt, gt = _inproj_odd_call(x, mod[layer], w_qko, wvt, wgt, od_gate_b[j][gate_perm])
            y = _mlstm_call(q, k, vt, o, gt, od_conv_w[j], od_conv_b[j].reshape(1, -1),
                            od_head_gain[j].reshape(1, -1))
            w_out = od_w_out[j].astype(BF16)
        x = _post_call(x, y, mod[layer], w_out, ffn_w1[layer].astype(BF16), ffn_w3[layer].astype(BF16),
                       ffn_w2[layer].astype(BF16))
    return x
```
